```python
import jax, jax.numpy as jnp
from jax import lax
import numpy as np

D_MODEL = 4096
BATCH = 2
SEQ = 8192
DEPTH = 4

N_MIXERS = 4
EPS = 1e-6

SSD_EXPAND = 2
SSD_D_INNER = SSD_EXPAND * D_MODEL
SSD_HEAD_DIM = 64
SSD_HEADS = SSD_D_INNER // SSD_HEAD_DIM
SSD_GROUPS = 8
SSD_STATE = 128
SSD_CONV = 4
SSD_CHUNK = 128
SSD_XBC = SSD_D_INNER + 2 * SSD_GROUPS * SSD_STATE
SSD_IN = SSD_D_INNER + SSD_XBC + SSD_HEADS

SC_CONV = 3

POOL_WINDOWS = (2, 4, 8, 16)
POOL_N_GROUPS = len(POOL_WINDOWS)
POOL_GROUP = D_MODEL // POOL_N_GROUPS

RET_HEADS = 16
RET_QK_DIM = D_MODEL // RET_HEADS
RET_V_DIM = 2 * RET_QK_DIM
RET_CHUNK = 128
RET_IN = 2 * RET_HEADS * RET_QK_DIM + 2 * RET_HEADS * RET_V_DIM
ROPE_BASE = 10000.0

D_FF = 11008
N_EXPERTS = 8
TOP_K = 2
D_FF_EXPERT = 1536

kernel_name = "hybrid_ssd_shortconv_pool_retention_moe"


def rmsnorm(x, g):
    xf = x.astype(jnp.float32)
    y = xf * lax.rsqrt(jnp.mean(xf * xf, axis=-1, keepdims=True) + EPS)
    return (y * g.astype(jnp.float32)).astype(x.dtype)


def causal_dwconv(x, w):
    k, c = w.shape
    return lax.conv_general_dilated(
        x, w[:, None, :].astype(x.dtype), window_strides=(1,), padding=[(k - 1, 0)],
        dimension_numbers=("NWC", "WIO", "NWC"), feature_group_count=c)


def to_chunks(t, chunk):
    b, s = t.shape[:2]
    return jnp.moveaxis(t.reshape(b, s // chunk, chunk, *t.shape[2:]), 1, 0)


def from_chunks(t):
    t = jnp.moveaxis(t, 0, 1)
    return t.reshape(t.shape[0], t.shape[1] * t.shape[2], *t.shape[3:])


def ssd_chunked_scan(xdt, a, bm, cm):
    b, s, g, hg, p = xdt.shape
    n = bm.shape[-1]
    causal = jnp.tril(jnp.ones((SSD_CHUNK, SSD_CHUNK), bool))[None, :, :, None, None]

    def step(h, inp):
        xc, ac, bc, cc = inp
        acum = jnp.cumsum(ac, axis=1)
        seg = acum[:, :, None] - acum[:, None]
        decay = jnp.exp(jnp.where(causal, seg, -jnp.inf))
        cb = jnp.einsum("bign,bjgn->bgij", cc, bc)
        y_diag = jnp.einsum("bgij,bijgh,bjghp->bighp", cb, decay, xc)
        y_off = jnp.einsum("bign,bghpn->bighp", cc, h) * jnp.exp(acum)[..., None]
        decay_end = jnp.exp(acum[:, -1:] - acum)
        h_new = h * jnp.exp(acum[:, -1])[..., None, None] + jnp.einsum(
            "bjgn,bjgh,bjghp->bghpn", bc, decay_end, xc)
        return h_new, y_diag + y_off

    h0 = jnp.zeros((b, g, hg, p, n), jnp.float32)
    _, y = lax.scan(step, h0, (to_chunks(xdt, SSD_CHUNK), to_chunks(a, SSD_CHUNK),
                               to_chunks(bm, SSD_CHUNK), to_chunks(cm, SSD_CHUNK)))
    return from_chunks(y)


def mamba2_mixer(u, w_in, conv_w, conv_b, dt_bias, a_log, d_skip, norm_g, w_out):
    b, s, _ = u.shape
    f32 = jnp.float32
    hg = SSD_HEADS // SSD_GROUPS
    z, xbc, dt = jnp.split(u @ w_in, [SSD_D_INNER, SSD_D_INNER + SSD_XBC], axis=-1)
    xbc = jax.nn.silu(causal_dwconv(xbc, conv_w) + conv_b.astype(xbc.dtype))
    xs, bm, cm = jnp.split(xbc, [SSD_D_INNER, SSD_D_INNER + SSD_GROUPS * SSD_STATE], axis=-1)
    dt = jax.nn.softplus(dt.astype(f32) + dt_bias.astype(f32)).reshape(b, s, SSD_GROUPS, hg)
    a = dt * (-jnp.exp(a_log.astype(f32))).reshape(SSD_GROUPS, hg)
    xh = xs.astype(f32).reshape(b, s, SSD_GROUPS, hg, SSD_HEAD_DIM)
    y = ssd_chunked_scan(xh * dt[..., None], a,
                         bm.astype(f32).reshape(b, s, SSD_GROUPS, SSD_STATE),
                         cm.astype(f32).reshape(b, s, SSD_GROUPS, SSD_STATE))
    y = y + xh * d_skip.astype(f32).reshape(SSD_GROUPS, hg, 1)
    y = y.reshape(b, s, SSD_D_INNER) * jax.nn.silu(z.astype(f32))
    yg = y.reshape(b, s, SSD_GROUPS, -1)
    yg = yg * lax.rsqrt(jnp.mean(yg * yg, axis=-1, keepdims=True) + EPS)
    y = yg.reshape(b, s, SSD_D_INNER) * norm_g.astype(f32)
    return y.astype(u.dtype) @ w_out


def short_conv_mixer(u, w_in, conv_w, w_out):
    gate_b, gate_c, h = jnp.split(u @ w_in, 3, axis=-1)
    return (gate_b * causal_dwconv(gate_c * h, conv_w)) @ w_out


def pool_mixer(u, w_group, scale):
    b, s, d = u.shape
    f32 = jnp.float32
    uf = u.astype(f32).reshape(b, s, POOL_N_GROUPS, POOL_GROUP)
    cs = jnp.cumsum(uf, axis=1)
    t = jnp.arange(1, s + 1, dtype=f32)[None, :, None]
    outs = []
    for gi, w in enumerate(POOL_WINDOWS):
        c = cs[:, :, gi]
        lower = jnp.concatenate([jnp.zeros((b, w, POOL_GROUP), f32), c[:, :s - w]], axis=1)
        mean = (c - lower) / jnp.minimum(t, float(w))
        outs.append(mean - uf[:, :, gi])
    pooled = jnp.stack(outs, axis=2)
    y = jnp.einsum("bsgc,gcd->bsgd", pooled, w_group.astype(f32)).reshape(b, s, d)
    return (y * scale.astype(f32)).astype(u.dtype)


def rotary(t, pos):
    half = t.shape[-1] // 2
    inv = ROPE_BASE ** (-jnp.arange(half, dtype=jnp.float32) / half)
    ang = pos[:, None] * inv[None]
    cos, sin = jnp.cos(ang)[None, :, None], jnp.sin(ang)[None, :, None]
    t1, t2 = t[..., :half], t[..., half:]
    return jnp.concatenate([t1 * cos - t2 * sin, t1 * sin + t2 * cos], axis=-1)


def retention_mixer(u, w_qkvg, w_out):
    b, s, _ = u.shape
    f32 = jnp.float32
    H, dk, dv, Q = RET_HEADS, RET_QK_DIM, RET_V_DIM, RET_CHUNK
    q, k, v, g = jnp.split(u @ w_qkvg, [H * dk, 2 * H * dk, 2 * H * dk + H * dv], axis=-1)
    pos = jnp.arange(s, dtype=f32)
    q = rotary(q.astype(f32).reshape(b, s, H, dk), pos)
    k = rotary(k.astype(f32).reshape(b, s, H, dk), pos) * (dk ** -0.5)
    v = v.astype(f32).reshape(b, s, H, dv)
    log_gamma = jnp.log1p(-jnp.exp2(-5.0 - jnp.arange(H, dtype=f32)))
    idx = jnp.arange(Q, dtype=f32)
    rel = idx[:, None] - idx[None, :]
    inner = jnp.exp(jnp.where(rel[None] >= 0, rel[None] * log_gamma[:, None, None], -jnp.inf))
    q_decay = jnp.exp((idx + 1.0)[:, None] * log_gamma[None])[None, :, :, None]
    k_decay = jnp.exp((Q - 1.0 - idx)[:, None] * log_gamma[None])[None, :, :, None]
    chunk_decay = jnp.exp(Q * log_gamma)[None, :, None, None]

    def step(state, inp):
        qc, kc, vc = inp
        scores = jnp.einsum("bihd,bjhd->bhij", qc, kc) * inner
        y_in = jnp.einsum("bhij,bjhe->bihe", scores, vc)
        y_cross = jnp.einsum("bihd,bhde->bihe", qc, state) * q_decay
        state = state * chunk_decay + jnp.einsum("bjhd,bjhe->bhde", kc * k_decay, vc)
        return state, y_in + y_cross

    s0 = jnp.zeros((b, H, dk, dv), f32)
    _, y = lax.scan(step, s0, (to_chunks(q, Q), to_chunks(k, Q), to_chunks(v, Q)))
    y = from_chunks(y)
    y = y * lax.rsqrt(jnp.mean(y * y, axis=-1, keepdims=True) + EPS)
    y = y.reshape(b, s, H * dv) * jax.nn.silu(g.astype(f32))
    return y.astype(u.dtype) @ w_out


def swiglu(u, w_gate, w_up, w_down):
    return (jax.nn.silu(u @ w_gate) * (u @ w_up)) @ w_down


def moe_swiglu(u, w_router, w_gate, w_up, w_down):
    b, s, d = u.shape
    t = u.reshape(b * s, d)
    logits = (t @ w_router).astype(jnp.float32)
    top_val, top_idx = lax.top_k(logits, TOP_K)
    top_w = jax.nn.softmax(top_val, axis=-1)
    gates = jnp.sum(jax.nn.one_hot(top_idx, N_EXPERTS, dtype=jnp.float32) * top_w[..., None], axis=1)
    out = jnp.zeros((b * s, d), jnp.float32)
    for e in range(N_EXPERTS):
        out = out + gates[:, e:e + 1] * swiglu(t, w_gate[e], w_up[e], w_down[e]).astype(jnp.float32)
    return out.astype(u.dtype).reshape(b, s, d)


def setup_inputs(seed: int = 0) -> dict:
    key = jax.random.key(seed)
    ks = iter(jax.random.split(key, 40))
    f32 = jnp.float32

    def dense(shape, fan_in):
        return jax.random.normal(next(ks), shape, f32) * (fan_in ** -0.5)

    def gain(shape, noise=0.02):
        return 1.0 + noise * jax.random.normal(next(ks), shape, f32)

    x = jax.random.normal(next(ks), (BATCH, SEQ, D_MODEL), f32)
    norm_mix = gain((DEPTH, D_MODEL))
    norm_ffn = gain((DEPTH, D_MODEL))
    norm_final = gain((D_MODEL,))
    ssd_w_in = dense((D_MODEL, SSD_IN), D_MODEL)
    ssd_conv_w = dense((SSD_CONV, SSD_XBC), SSD_CONV)
    ssd_conv_b = 0.01 * jax.random.normal(next(ks), (SSD_XBC,), f32)
    dt0 = jnp.exp(jax.random.uniform(next(ks), (SSD_HEADS,), f32, minval=np.log(1e-3), maxval=np.log(1e-1)))
    ssd_dt_bias = dt0 + jnp.log(-jnp.expm1(-dt0))
    ssd_a_log = jnp.log(jax.random.uniform(next(ks), (SSD_HEADS,), f32, minval=1.0, maxval=16.0))
    ssd_d = gain((SSD_HEADS,), 0.1)
    ssd_norm = gain((SSD_D_INNER,))
    ssd_w_out = dense((SSD_D_INNER, D_MODEL), SSD_D_INNER)
    sc_w_in = dense((D_MODEL, 3 * D_MODEL), D_MODEL)
    sc_conv_w = dense((SC_CONV, D_MODEL), SC_CONV)
    sc_w_out = dense((D_MODEL, D_MODEL), D_MODEL)
    pool_w = dense((POOL_N_GROUPS, POOL_GROUP, POOL_GROUP), POOL_GROUP)
    pool_scale = gain((D_MODEL,))
    ret_w_qkvg = dense((D_MODEL, RET_IN), D_MODEL)
    ret_w_out = dense((RET_HEADS * RET_V_DIM, D_MODEL), RET_HEADS * RET_V_DIM)
    ffn0_w_gate = dense((D_MODEL, D_FF), D_MODEL)
    ffn0_w_up = dense((D_MODEL, D_FF), D_MODEL)
    ffn0_w_down = dense((D_FF, D_MODEL), D_FF)
    moe1_router = dense((D_MODEL, N_EXPERTS), D_MODEL)
    moe1_w_gate = dense((N_EXPERTS, D_MODEL, D_FF_EXPERT), D_MODEL)
    moe1_w_up = dense((N_EXPERTS, D_MODEL, D_FF_EXPERT), D_MODEL)
    moe1_w_down = dense((N_EXPERTS, D_FF_EXPERT, D_MODEL), D_FF_EXPERT)
    ffn2_w_gate = dense((D_MODEL, D_FF), D_MODEL)
    ffn2_w_up = dense((D_MODEL, D_FF), D_MODEL)
    ffn2_w_down = dense((D_FF, D_MODEL), D_FF)
    moe3_router = dense((D_MODEL, N_EXPERTS), D_MODEL)
    moe3_w_gate = dense((N_EXPERTS, D_MODEL, D_FF_EXPERT), D_MODEL)
    moe3_w_up = dense((N_EXPERTS, D_MODEL, D_FF_EXPERT), D_MODEL)
    moe3_w_down = dense((N_EXPERTS, D_FF_EXPERT, D_MODEL), D_FF_EXPERT)
    return {
        "x": x, "norm_mix": norm_mix, "norm_ffn": norm_ffn, "norm_final": norm_final,
        "ssd_w_in": ssd_w_in, "ssd_conv_w": ssd_conv_w, "ssd_conv_b": ssd_conv_b,
        "ssd_dt_bias": ssd_dt_bias, "ssd_a_log": ssd_a_log, "ssd_d": ssd_d,
        "ssd_norm": ssd_norm, "ssd_w_out": ssd_w_out,
        "sc_w_in": sc_w_in, "sc_conv_w": sc_conv_w, "sc_w_out": sc_w_out,
        "pool_w": pool_w, "pool_scale": pool_scale,
        "ret_w_qkvg": ret_w_qkvg, "ret_w_out": ret_w_out,
        "ffn0_w_gate": ffn0_w_gate, "ffn0_w_up": ffn0_w_up, "ffn0_w_down": ffn0_w_down,
        "moe1_router": moe1_router, "moe1_w_gate": moe1_w_gate, "moe1_w_up": moe1_w_up,
        "moe1_w_down": moe1_w_down,
        "ffn2_w_gate": ffn2_w_gate, "ffn2_w_up": ffn2_w_up, "ffn2_w_down": ffn2_w_down,
        "moe3_router": moe3_router, "moe3_w_gate": moe3_w_gate, "moe3_w_up": moe3_w_up,
        "moe3_w_down": moe3_w_down,
    }


def reference(x, norm_mix, norm_ffn, norm_final,
              ssd_w_in, ssd_conv_w, ssd_conv_b, ssd_dt_bias, ssd_a_log, ssd_d, ssd_norm, ssd_w_out,
              sc_w_in, sc_conv_w, sc_w_out,
              pool_w, pool_scale,
              ret_w_qkvg, ret_w_out,
              ffn0_w_gate, ffn0_w_up, ffn0_w_down,
              moe1_router, moe1_w_gate, moe1_w_up, moe1_w_down,
              ffn2_w_gate, ffn2_w_up, ffn2_w_down,
              moe3_router, moe3_w_gate, moe3_w_up, moe3_w_down):
    mixers = [
        lambda u: mamba2_mixer(u, ssd_w_in, ssd_conv_w, ssd_conv_b, ssd_dt_bias, ssd_a_log,
                               ssd_d, ssd_norm, ssd_w_out),
        lambda u: short_conv_mixer(u, sc_w_in, sc_conv_w, sc_w_out),
        lambda u: pool_mixer(u, pool_w, pool_scale),
        lambda u: retention_mixer(u, ret_w_qkvg, ret_w_out),
    ]
    channel_mixers = [
        lambda u: swiglu(u, ffn0_w_gate, ffn0_w_up, ffn0_w_down),
        lambda u: moe_swiglu(u, moe1_router, moe1_w_gate, moe1_w_up, moe1_w_down),
        lambda u: swiglu(u, ffn2_w_gate, ffn2_w_up, ffn2_w_down),
        lambda u: moe_swiglu(u, moe3_router, moe3_w_gate, moe3_w_up, moe3_w_down),
    ]
    h = x
    for i in range(DEPTH):
        h = h + mixers[i % N_MIXERS](rmsnorm(h, norm_mix[i]))
        h = h + channel_mixers[i](rmsnorm(h, norm_ffn[i]))
    return rmsnorm(h, norm_final)
```

```python
import functools

import jax
import jax.numpy as jnp
from jax import lax
from jax.experimental import pallas as pl
from jax.experimental.pallas import tpu as pltpu

F32 = jnp.float32
BF16 = jnp.bfloat16
EPS = 1e-6
V7X_VMEM_LIMIT_BYTES = 60 * 1024 * 1024
LANES = 128

SSD_HEAD_DIM = 64
SSD_GROUPS = 8
SSD_STATE = 128
SSD_CHUNK = 128
POOL_WINDOWS = (2, 4, 8, 16)
POOL_HALO = 16
RET_HEADS = 16
RET_CHUNK = 128
ROPE_BASE = 10000.0
N_EXPERTS = 8


def _params(*sem):
    return pltpu.CompilerParams(dimension_semantics=sem, vmem_limit_bytes=V7X_VMEM_LIMIT_BYTES)


def _silu(x):
    return x / (1.0 + jnp.exp(-x))


def _dot(a, b):
    return jnp.dot(a, b, preferred_element_type=F32)


def _dot_nt(a, b):
    return lax.dot_general(a, b, (((1,), (1,)), ((), ())), preferred_element_type=F32)


def _split3(x):
    hi = x.astype(BF16)
    r1 = x - hi.astype(F32)
    mid = r1.astype(BF16)
    lo = (r1 - mid.astype(F32)).astype(BF16)
    return hi, mid, lo


def _rms(x, g):
    return x * lax.rsqrt(jnp.mean(x * x, axis=-1, keepdims=True) + EPS) * g


def _rmsnorm_kernel(x_ref, g_ref, o_ref):
    o_ref[...] = _rms(x_ref[...], g_ref[...]).astype(o_ref.dtype)


def rmsnorm(x, g, out_dtype, tm=512):
    t, d = x.shape
    tm = min(tm, t)
    return pl.pallas_call(
        _rmsnorm_kernel, grid=(t // tm,),
        in_specs=[pl.BlockSpec((tm, d), lambda i: (i, 0)), pl.BlockSpec((1, d), lambda i: (0, 0))],
        out_specs=pl.BlockSpec((tm, d), lambda i: (i, 0)),
        out_shape=jax.ShapeDtypeStruct((t, d), out_dtype),
        compiler_params=_params("parallel"), name="rmsnorm",
    )(x, g.reshape(1, d))


def _rmsnorm_router_kernel(x_ref, g_ref, wr_ref, o_ref, gates_ref, *, n_experts):
    xn = _rms(x_ref[...], g_ref[...])
    o_ref[...] = xn.astype(o_ref.dtype)
    xh, xm, _ = _split3(xn)
    wh, wm, _ = _split3(wr_ref[...])
    logits = _dot(xh, wh) + _dot(xm, wh) + _dot(xh, wm)
    lane = lax.broadcasted_iota(jnp.int32, logits.shape, 1).astype(F32)
    neg = jnp.float32(-jnp.inf)
    l1 = jnp.where(lane < n_experts, logits, neg)
    m1 = jnp.max(l1, axis=-1, keepdims=True)
    i1 = jnp.min(jnp.where(l1 == m1, lane, float(LANES)), axis=-1, keepdims=True)
    l2 = jnp.where(lane == i1, neg, l1)
    m2 = jnp.max(l2, axis=-1, keepdims=True)
    i2 = jnp.min(jnp.where(l2 == m2, lane, float(LANES)), axis=-1, keepdims=True)
    e2 = jnp.exp(m2 - m1)
    w1 = 1.0 / (1.0 + e2)
    w2 = e2 / (1.0 + e2)
    gates_ref[...] = jnp.where(lane == i1, w1, 0.0) + jnp.where(lane == i2, w2, 0.0)


def rmsnorm_router(x, g, w_router, tm=512):
    t, d = x.shape
    tm = min(tm, t)
    n_experts = w_router.shape[1]
    wr = jnp.pad(w_router, ((0, 0), (0, LANES - n_experts)))
    return pl.pallas_call(
        functools.partial(_rmsnorm_router_kernel, n_experts=n_experts), grid=(t // tm,),
        in_specs=[pl.BlockSpec((tm, d), lambda i: (i, 0)), pl.BlockSpec((1, d), lambda i: (0, 0)),
                  pl.BlockSpec((d, LANES), lambda i: (0, 0))],
        out_specs=[pl.BlockSpec((tm, d), lambda i: (i, 0)), pl.BlockSpec((tm, LANES), lambda i: (i, 0))],
        out_shape=[jax.ShapeDtypeStruct((t, d), BF16), jax.ShapeDtypeStruct((t, LANES), F32)],
        compiler_params=_params("parallel"), name="rmsnorm_router",
    )(x, g.reshape(1, d), wr)


def _mm_kernel(a_ref, b_ref, o_ref):
    o_ref[...] = _dot(a_ref[...], b_ref[...]).astype(o_ref.dtype)


def matmul(a, b, out_dtype, tm, tn):
    m, k = a.shape
    n = b.shape[1]
    tm, tn = min(tm, m), min(tn, n)
    return pl.pallas_call(
        _mm_kernel, grid=(m // tm, n // tn),
        in_specs=[pl.BlockSpec((tm, k), lambda i, j: (i, 0)), pl.BlockSpec((k, tn), lambda i, j: (0, j))],
        out_specs=pl.BlockSpec((tm, tn), lambda i, j: (i, j)),
        out_shape=jax.ShapeDtypeStruct((m, n), out_dtype),
        compiler_params=_params("parallel", "parallel"), name="matmul",
    )(a, b)


def _mm_res_kernel(a_ref, b_ref, r_ref, o_ref):
    o_ref[...] = r_ref[...] + _dot(a_ref[...], b_ref[...])


def matmul_residual(a, b, res, tm, tn):
    m, k = a.shape
    n = b.shape[1]
    tm, tn = min(tm, m), min(tn, n)
    return pl.pallas_call(
        _mm_res_kernel, grid=(m // tm, n // tn),
        in_specs=[pl.BlockSpec((tm, k), lambda i, j: (i, 0)), pl.BlockSpec((k, tn), lambda i, j: (0, j)),
                  pl.BlockSpec((tm, tn), lambda i, j: (i, j))],
        out_specs=pl.BlockSpec((tm, tn), lambda i, j: (i, j)),
        out_shape=jax.ShapeDtypeStruct((m, n), F32),
        compiler_params=_params("parallel", "parallel"), name="matmul_residual",
    )(a, b, res)


def _swiglu_kernel(a_ref, wg_ref, wu_ref, o_ref):
    a = a_ref[...]
    g = _dot(a, wg_ref[...])
    u = _dot(a, wu_ref[...])
    o_ref[...] = (_silu(g) * u).astype(o_ref.dtype)


def swiglu_up(a, w_gate, w_up, tm, tn):
    m, k = a.shape
    n = w_gate.shape[1]
    tm, tn = min(tm, m), min(tn, n)
    return pl.pallas_call(
        _swiglu_kernel, grid=(m // tm, n // tn),
        in_specs=[pl.BlockSpec((tm, k), lambda i, j: (i, 0)), pl.BlockSpec((k, tn), lambda i, j: (0, j)),
                  pl.BlockSpec((k, tn), lambda i, j: (0, j))],
        out_specs=pl.BlockSpec((tm, tn), lambda i, j: (i, j)),
        out_shape=jax.ShapeDtypeStruct((m, n), BF16),
        compiler_params=_params("parallel", "parallel"), name="swiglu_up",
    )(a, w_gate, w_up)


def _moe_up_kernel(a_ref, gates_ref, wg_ref, wu_ref, o_ref):
    e = pl.program_id(1)
    a = a_ref[...]
    g = _dot(a, wg_ref[...])
    u = _dot(a, wu_ref[...])
    gates = gates_ref[...]
    lane = lax.broadcasted_iota(jnp.int32, gates.shape, 1)
    gate_e = jnp.sum(jnp.where(lane == e, gates, 0.0), axis=1, keepdims=True)
    o_ref[...] = (_silu(g) * u * gate_e).astype(o_ref.dtype)


def moe_up(a, gates, w_gate, w_up, tm, tn):
    m, k = a.shape
    n_e, _, f = w_gate.shape
    tm, tn = min(tm, m), min(tn, f)
    nj = f // tn
    return pl.pallas_call(
        _moe_up_kernel, grid=(m // tm, n_e, nj),
        in_specs=[pl.BlockSpec((tm, k), lambda i, e, j: (i, 0)),
                  pl.BlockSpec((tm, LANES), lambda i, e, j: (i, 0)),
                  pl.BlockSpec((None, k, tn), lambda i, e, j: (e, 0, j)),
                  pl.BlockSpec((None, k, tn), lambda i, e, j: (e, 0, j))],
        out_specs=pl.BlockSpec((tm, tn), lambda i, e, j: (i, e * nj + j)),
        out_shape=jax.ShapeDtypeStruct((m, n_e * f), BF16),
        compiler_params=_params("parallel", "parallel", "parallel"), name="moe_up",
    )(a, gates, w_gate, w_up)


def _softplus(x):
    return jnp.maximum(x, 0.0) + jnp.log1p(jnp.exp(-jnp.abs(x)))


def _conv_silu_kernel(x_ref, halo_ref, w_ref, b_ref, o_ref, buf_ref, *, seq_tiles, k_conv):
    i = pl.program_id(0)
    ts = x_ref.shape[0]
    hl = halo_ref.shape[0]
    x = x_ref[...].astype(F32)
    halo = halo_ref[...].astype(F32)
    buf_ref[0:hl, :] = jnp.where(i % seq_tiles == 0, 0.0, halo)
    buf_ref[hl:hl + ts, :] = x
    acc = x * w_ref[k_conv - 1:k_conv, :] + b_ref[...]
    for s in range(1, k_conv):
        acc = acc + buf_ref[hl - s:hl - s + ts, :] * w_ref[k_conv - 1 - s:k_conv - s, :]
    o_ref[...] = _silu(acc).astype(o_ref.dtype)


def conv_silu(zx, col0, n_cols, conv_w, conv_b, seq, ts=512, tc=1024, hl=16):
    t = zx.shape[0]
    k_conv = conv_w.shape[0]
    ts, tc = min(ts, seq), min(tc, n_cols)
    cb0 = col0 // tc
    return pl.pallas_call(
        functools.partial(_conv_silu_kernel, seq_tiles=seq // ts, k_conv=k_conv),
        grid=(t // ts, n_cols // tc),
        in_specs=[pl.BlockSpec((ts, tc), lambda i, j: (i, cb0 + j)),
                  pl.BlockSpec((hl, tc), lambda i, j: (jnp.maximum(i * (ts // hl) - 1, 0), cb0 + j)),
                  pl.BlockSpec((k_conv, tc), lambda i, j: (0, j)),
                  pl.BlockSpec((1, tc), lambda i, j: (0, j))],
        out_specs=pl.BlockSpec((ts, tc), lambda i, j: (i, j)),
        out_shape=jax.ShapeDtypeStruct((t, n_cols), BF16),
        scratch_shapes=[pltpu.VMEM((hl + ts, tc), F32)],
        compiler_params=_params("parallel", "parallel"), name="conv_silu",
    )(zx, zx, conv_w, conv_b.reshape(1, n_cols))


def _ssd_scan_kernel(x_ref, b_ref, c_ref, z_ref, dt_ref, dtb_ref, alog_ref, dsk_ref, ng_ref, e_ref,
                     o_ref, state_ref, acumT_ref, dtT_ref, y_ref, *, hg):
    g = pl.program_id(1)
    c = pl.program_id(2)
    q = x_ref.shape[0]

    @pl.when(c == 0)
    def _():
        state_ref[...] = jnp.zeros_like(state_ref)

    dt = _softplus(dt_ref[...] + dtb_ref[...])
    a = dt * (-jnp.exp(alog_ref[...]))
    row = lax.broadcasted_iota(jnp.int32, (q, q), 0)
    col = lax.broadcasted_iota(jnp.int32, (q, q), 1)
    causal = row >= col
    tri = jnp.where(causal, 1.0, 0.0).astype(BF16)
    a_hi, a_mid, a_lo = _split3(a)
    acum = _dot(tri, a_hi) + _dot(tri, a_mid) + _dot(tri, a_lo)
    acumT_ref[...] = acum.T
    dtT_ref[...] = dt.T
    a_last = acum[q - 1:q, :]
    w_in = jnp.exp(a_last - acum) * dt
    e_a = jnp.exp(acum)
    expd = _dot(jnp.concatenate([w_in, e_a], axis=0).astype(BF16), e_ref[...])
    w_exp = expd[:q, :]
    e_exp = expd[q:, :]

    bm = b_ref[...]
    cm = c_ref[...]
    cb = _dot_nt(cm, bm)
    lane = lax.broadcasted_iota(jnp.int32, (q, LANES), 1)
    heads_per_tile = LANES // SSD_HEAD_DIM
    for p in range(hg // heads_per_tile):
        ms = []
        for hh in range(heads_per_tile):
            hidx = g * hg + p * heads_per_tile + hh
            colv = jnp.sum(jnp.where(lane == hidx, acum, 0.0), axis=1, keepdims=True)
            rowv = acumT_ref[pl.ds(hidx, 1), :]
            dtr = dtT_ref[pl.ds(hidx, 1), :]
            decay = jnp.where(causal, jnp.exp(jnp.minimum(colv - rowv, 0.0)), 0.0)
            ms.append((cb * decay * dtr).astype(BF16))
        m_cat = jnp.concatenate(ms, axis=1)
        xp = x_ref[:, p * LANES:(p + 1) * LANES].astype(F32)
        x_bd = jnp.concatenate(
            [jnp.where((lane // SSD_HEAD_DIM) == hh, xp, 0.0).astype(BF16) for hh in range(heads_per_tile)],
            axis=0)
        y_ref[:, p * LANES:(p + 1) * LANES] = _dot(m_cat, x_bd)

    st = state_ref[...]
    xf = x_ref[...].astype(F32)
    y_off = _dot(cm, st.astype(BF16)) * e_exp
    xw = (xf * w_exp).astype(BF16)
    bt = bm.astype(F32).T.astype(BF16)
    state_ref[...] = st * e_exp[q - 1:q, :] + _dot(bt, xw)
    y = y_ref[...] + y_off + xf * dsk_ref[...]
    y = y * _silu(z_ref[...].astype(F32))
    y = y * lax.rsqrt(jnp.mean(y * y, axis=-1, keepdims=True) + EPS) * ng_ref[...]
    o_ref[...] = y.astype(o_ref.dtype)


def ssd_scan(xbc, zx, dt_raw, dt_bias, a_log, d_skip, norm_g, batch, seq):
    t = xbc.shape[0]
    heads = dt_raw.shape[1]
    assert heads == LANES
    hg = heads // SSD_GROUPS
    gw = hg * SSD_HEAD_DIM
    d_inner = heads * SSD_HEAD_DIM
    q = SSD_CHUNK
    nc = seq // q
    n = SSD_STATE
    assert gw % LANES == 0 and n == LANES
    b_blk0 = d_inner // n
    c_blk0 = (d_inner + SSD_GROUPS * n) // n
    ch = jnp.arange(gw) // SSD_HEAD_DIM
    expand = (jnp.arange(heads)[None, :, None] ==
              (jnp.arange(SSD_GROUPS)[:, None, None] * hg + ch[None, None, :])).astype(BF16)
    d_exp = jnp.repeat(d_skip, SSD_HEAD_DIM).reshape(1, d_inner)
    rowmap = lambda b, g, c: (b * nc + c)
    return pl.pallas_call(
        functools.partial(_ssd_scan_kernel, hg=hg),
        grid=(batch, SSD_GROUPS, nc),
        in_specs=[pl.BlockSpec((q, gw), lambda b, g, c: (rowmap(b, g, c), g)),
                  pl.BlockSpec((q, n), lambda b, g, c: (rowmap(b, g, c), b_blk0 + g)),
                  pl.BlockSpec((q, n), lambda b, g, c: (rowmap(b, g, c), c_blk0 + g)),
                  pl.BlockSpec((q, gw), lambda b, g, c: (rowmap(b, g, c), g)),
                  pl.BlockSpec((q, heads), lambda b, g, c: (rowmap(b, g, c), 0)),
                  pl.BlockSpec((1, heads), lambda b, g, c: (0, 0)),
                  pl.BlockSpec((1, heads), lambda b, g, c: (0, 0)),
                  pl.BlockSpec((1, gw), lambda b, g, c: (0, g)),
                  pl.BlockSpec((1, gw), lambda b, g, c: (0, g)),
                  pl.BlockSpec((None, heads, gw), lambda b, g, c: (g, 0, 0))],
        out_specs=pl.BlockSpec((q, gw), lambda b, g, c: (rowmap(b, g, c), g)),
        out_shape=jax.ShapeDtypeStruct((t, d_inner), BF16),
        scratch_shapes=[pltpu.VMEM((n, gw), F32), pltpu.VMEM((heads, q), F32), pltpu.VMEM((heads, q), F32),
                        pltpu.VMEM((q, gw), F32)],
        compiler_params=_params("parallel", "parallel", "arbitrary"), name="ssd_scan",
    )(xbc, xbc, xbc, zx, dt_raw, dt_bias.reshape(1, heads), a_log.reshape(1, heads), d_exp,
      norm_g.reshape(1, d_inner), expand)


def _shortconv_in_kernel(a_ref, wb_ref, wc_ref, wh_ref, cw_ref, o_ref, carry_ref, buf_ref, *, seq_tiles, k_conv):
    i = pl.program_id(0)
    j = pl.program_id(1)
    tm = a_ref.shape[0]
    hl = carry_ref.shape[1]
    a = a_ref[...]
    gate_b = _dot(a, wb_ref[...])
    v = _dot(a, wc_ref[...]) * _dot(a, wh_ref[...])
    @pl.when(i % seq_tiles == 0)
    def _():
        buf_ref[0:hl, :] = jnp.zeros((hl, v.shape[1]), F32)

    @pl.when(i % seq_tiles != 0)
    def _():
        buf_ref[0:hl, :] = carry_ref[j]

    buf_ref[hl:hl + tm, :] = v
    carry_ref[j] = v[tm - hl:, :]
    acc = v * cw_ref[k_conv - 1:k_conv, :]
    for s in range(1, k_conv):
        acc = acc + buf_ref[hl - s:hl - s + tm, :] * cw_ref[k_conv - 1 - s:k_conv - s, :]
    o_ref[...] = (gate_b * acc).astype(o_ref.dtype)


def shortconv_in(a, w_b, w_c, w_h, conv_w, seq, tm=1024, tn=256, hl=8):
    m, k = a.shape
    n = w_b.shape[1]
    k_conv = conv_w.shape[0]
    tm, tn = min(tm, seq), min(tn, n)
    wspec = pl.BlockSpec((k, tn), lambda i, j: (0, j))
    return pl.pallas_call(
        functools.partial(_shortconv_in_kernel, seq_tiles=seq // tm, k_conv=k_conv),
        grid=(m // tm, n // tn),
        in_specs=[pl.BlockSpec((tm, k), lambda i, j: (i, 0)), wspec, wspec, wspec,
                  pl.BlockSpec((k_conv, tn), lambda i, j: (0, j))],
        out_specs=pl.BlockSpec((tm, tn), lambda i, j: (i, j)),
        out_shape=jax.ShapeDtypeStruct((m, n), BF16),
        scratch_shapes=[pltpu.VMEM((n // tn, hl, tn), F32), pltpu.VMEM((hl + tm, tn), F32)],
        compiler_params=_params("arbitrary", "arbitrary"), name="shortconv_in",
    )(a, w_b, w_c, w_h, conv_w)


def _pool_kernel(h_ref, halo_ref, g_ref, w_ref, sc_ref, o_ref, buf_ref, *, seq, windows):
    i = pl.program_id(0)
    tm, d = h_ref.shape
    hl = halo_ref.shape[0]
    gw = d // len(windows)
    h = h_ref[...]
    g = g_ref[...]
    xn = _rms(h, g)
    row0 = i * tm
    seq_start = (row0 % seq) == 0
    buf_ref[0:hl, :] = jnp.where(seq_start, 0.0, _rms(halo_ref[...], g))
    buf_ref[hl:hl + tm, :] = xn
    pos = (row0 + lax.broadcasted_iota(jnp.int32, (tm, 1), 0)) % seq
    for gi, w in enumerate(windows):
        cs = slice(gi * gw, (gi + 1) * gw)
        tok = xn[:, cs]
        acc = tok
        for s in range(1, w):
            acc = acc + buf_ref[hl - s:hl - s + tm, cs]
        cnt = jnp.minimum(pos + 1, w).astype(F32)
        pooled = (acc / cnt - tok).astype(BF16)
        y = _dot(pooled, w_ref[gi])
        o_ref[:, cs] = h[:, cs] + y * sc_ref[:, cs]


def pool_mixer_residual(h, norm_g, w_group, scale, seq, tm=256):
    t, d = h.shape
    tm = min(tm, seq)
    hl = POOL_HALO
    n_g, gw, _ = w_group.shape
    return pl.pallas_call(
        functools.partial(_pool_kernel, seq=seq, windows=POOL_WINDOWS),
        grid=(t // tm,),
        in_specs=[pl.BlockSpec((tm, d), lambda i: (i, 0)),
                  pl.BlockSpec((hl, d), lambda i: (jnp.maximum(i * (tm // hl) - 1, 0), 0)),
                  pl.BlockSpec((1, d), lambda i: (0, 0)),
                  pl.BlockSpec((n_g, gw, gw), lambda i: (0, 0, 0)),
                  pl.BlockSpec((1, d), lambda i: (0, 0))],
        out_specs=pl.BlockSpec((tm, d), lambda i: (i, 0)),
        out_shape=jax.ShapeDtypeStruct((t, d), F32),
        scratch_shapes=[pltpu.VMEM((hl + tm, d), F32)],
        compiler_params=_params("parallel"), name="pool_mixer",
    )(h, h, norm_g.reshape(1, d), w_group, scale.reshape(1, d))


def _retention_kernel(q_ref, k_ref, v_ref, g_ref, cos_ref, sin_ref, inner_ref, qd_ref, kd_ref, cd_ref,
                      o_ref, state_ref):
    c = pl.program_id(2)

    @pl.when(c == 0)
    def _():
        state_ref[...] = jnp.zeros_like(state_ref)

    dk = q_ref.shape[1]
    half = dk // 2
    cos = cos_ref[...]
    sin = sin_ref[...]

    def rot(t):
        t1, t2 = t[:, :half], t[:, half:]
        return jnp.concatenate([t1 * cos - t2 * sin, t1 * sin + t2 * cos], axis=1)

    qr = rot(q_ref[...].astype(F32))
    kr = rot(k_ref[...].astype(F32)) * (dk ** -0.5)
    v = v_ref[...]
    scores = _dot_nt(qr.astype(BF16), kr.astype(BF16)) * inner_ref[...]
    st = state_ref[...]
    y = _dot(scores.astype(BF16), v) + _dot((qr * qd_ref[...]).astype(BF16), st.astype(BF16))
    kdt = (kr * kd_ref[...]).T.astype(BF16)
    state_ref[...] = st * cd_ref[...] + _dot(kdt, v)
    y = y * lax.rsqrt(jnp.mean(y * y, axis=-1, keepdims=True) + EPS)
    o_ref[...] = (y * _silu(g_ref[...].astype(F32))).astype(o_ref.dtype)


def retention(qkvg, batch, seq):
    t, n_in = qkvg.shape
    hh = RET_HEADS
    dk = n_in // (6 * hh)
    dv = 2 * dk
    q = RET_CHUNK
    nc = seq // q
    half = dk // 2
    pos = jnp.arange(seq, dtype=F32)
    inv = ROPE_BASE ** (-jnp.arange(half, dtype=F32) / half)
    ang = pos[:, None] * inv[None]
    cos, sin = jnp.cos(ang), jnp.sin(ang)
    log_gamma = jnp.log1p(-jnp.exp2(-5.0 - jnp.arange(hh, dtype=F32)))
    idx = jnp.arange(q, dtype=F32)
    rel = idx[:, None] - idx[None, :]
    inner = jnp.exp(jnp.where(rel[None] >= 0, rel[None] * log_gamma[:, None, None], -jnp.inf))
    q_decay = jnp.exp((idx + 1.0)[None, :] * log_gamma[:, None])[..., None]
    k_decay = jnp.exp((q - 1.0 - idx)[None, :] * log_gamma[:, None])[..., None]
    c_decay = jnp.exp(q * log_gamma)[:, None, None]
    rowmap = lambda b, h, c: b * nc + c
    return pl.pallas_call(
        _retention_kernel, grid=(batch, hh, nc),
        in_specs=[pl.BlockSpec((q, dk), lambda b, h, c: (rowmap(b, h, c), h)),
                  pl.BlockSpec((q, dk), lambda b, h, c: (rowmap(b, h, c), hh + h)),
                  pl.BlockSpec((q, dv), lambda b, h, c: (rowmap(b, h, c), hh + h)),
                  pl.BlockSpec((q, dv), lambda b, h, c: (rowmap(b, h, c), 2 * hh + h)),
                  pl.BlockSpec((q, half), lambda b, h, c: (c, 0)),
                  pl.BlockSpec((q, half), lambda b, h, c: (c, 0)),
                  pl.BlockSpec((None, q, q), lambda b, h, c: (h, 0, 0)),
                  pl.BlockSpec((None, q, 1), lambda b, h, c: (h, 0, 0)),
                  pl.BlockSpec((None, q, 1), lambda b, h, c: (h, 0, 0)),
                  pl.BlockSpec((None, 1, 1), lambda b, h, c: (h, 0, 0))],
        out_specs=pl.BlockSpec((q, dv), lambda b, h, c: (rowmap(b, h, c), h)),
        out_shape=jax.ShapeDtypeStruct((t, hh * dv), BF16),
        scratch_shapes=[pltpu.VMEM((dk, dv), F32)],
        compiler_params=_params("parallel", "parallel", "arbitrary"), name="retention",
    )(qkvg, qkvg, qkvg, qkvg, cos, sin, inner, q_decay, k_decay, c_decay)


def _dense_ffn(h, norm_g, w_gate, w_up, w_down):
    u = rmsnorm(h, norm_g, BF16)
    hid = swiglu_up(u, w_gate.astype(BF16), w_up.astype(BF16), tm=2048, tn=256)
    return matmul_residual(hid, w_down.astype(BF16), h, tm=512, tn=512)


def _moe_ffn(h, norm_g, w_router, w_gate, w_up, w_down):
    u, gates = rmsnorm_router(h, norm_g, w_router)
    hid = moe_up(u, gates, w_gate.astype(BF16), w_up.astype(BF16), tm=1024, tn=768)
    n_e, f, d = w_down.shape
    return matmul_residual(hid, w_down.astype(BF16).reshape(n_e * f, d), h, tm=512, tn=256)


def kernel(x, norm_mix, norm_ffn, norm_final, ssd_w_in, ssd_conv_w, ssd_conv_b, ssd_dt_bias, ssd_a_log, ssd_d, ssd_norm, ssd_w_out, sc_w_in, sc_conv_w, sc_w_out, pool_w, pool_scale, ret_w_qkvg, ret_w_out, ffn0_w_gate, ffn0_w_up, ffn0_w_down, moe1_router, moe1_w_gate, moe1_w_up, moe1_w_down, ffn2_w_gate, ffn2_w_up, ffn2_w_down, moe3_router, moe3_w_gate, moe3_w_up, moe3_w_down):
    batch, seq, d = x.shape
    t = batch * seq
    h = x.reshape(t, d)

    heads = ssd_dt_bias.shape[0]
    d_inner = heads * SSD_HEAD_DIM
    n_xbc = ssd_conv_w.shape[1]
    u = rmsnorm(h, norm_mix[0], BF16)
    zx = matmul(u, ssd_w_in[:, :d_inner + n_xbc].astype(BF16), BF16, tm=1024, tn=1024)
    dt_raw = matmul(u, ssd_w_in[:, d_inner + n_xbc:].astype(BF16), F32, tm=1024, tn=LANES)
    xbc = conv_silu(zx, d_inner, n_xbc, ssd_conv_w, ssd_conv_b, seq)
    y = ssd_scan(xbc, zx, dt_raw, ssd_dt_bias, ssd_a_log, ssd_d, ssd_norm, batch, seq)
    h = matmul_residual(y, ssd_w_out.astype(BF16), h, tm=1024, tn=256)
    h = _dense_ffn(h, norm_ffn[0], ffn0_w_gate, ffn0_w_up, ffn0_w_down)

    u = rmsnorm(h, norm_mix[1], BF16)
    w_in = sc_w_in.astype(BF16)
    y = shortconv_in(u, w_in[:, :d], w_in[:, d:2 * d], w_in[:, 2 * d:], sc_conv_w, seq)
    h = matmul_residual(y, sc_w_out.astype(BF16), h, tm=1024, tn=512)
    h = _moe_ffn(h, norm_ffn[1], moe1_router, moe1_w_gate, moe1_w_up, moe1_w_down)

    h = pool_mixer_residual(h, norm_mix[2], pool_w.astype(BF16), pool_scale, seq)
    h = _dense_ffn(h, norm_ffn[2], ffn2_w_gate, ffn2_w_up, ffn2_w_down)

    u = rmsnorm(h, norm_mix[3], BF16)
    qkvg = matmul(u, ret_w_qkvg.astype(BF16), BF16, tm=1024, tn=1024)
    y = retention(qkvg, batch, seq)
    h = matmul_residual(y, ret_w_out.astype(BF16), h, tm=1024, tn=256)
    h = _moe_ffn(h, norm_ffn[3], moe3_router, moe3_w_gate, moe3_w_up, moe3_w_down)

    return rmsnorm(h, norm_final, F32).reshape(batch, seq, d)
```

```python
import functools

import jax
import jax.numpy as jnp
from jax import lax
from jax.experimental import pallas as pl
from jax.experimental.pallas import tpu as pltpu

F32 = jnp.float32
BF16 = jnp.bfloat16
EPS = 1e-6
V7X_VMEM_LIMIT_BYTES = 60 * 1024 * 1024
LANES = 128

SSD_HEAD_DIM = 64
SSD_GROUPS = 8
SSD_STATE = 128
SSD_CHUNK = 128
POOL_WINDOWS = (2, 4, 8, 16)
POOL_HALO = 16
RET_HEADS = 16
RET_CHUNK = 128
ROPE_BASE = 10000.0
N_EXPERTS = 8


def _params(*sem):
    return pltpu.CompilerParams(dimension_semantics=sem, vmem_limit_bytes=V7X_VMEM_LIMIT_BYTES)


def _silu(x):
    return x / (1.0 + jnp.exp(-x))


def _dot(a, b):
    return jnp.dot(a, b, preferred_element_type=F32)


def _dot_nt(a, b):
    return lax.dot_general(a, b, (((1,), (1,)), ((), ())), preferred_element_type=F32)


def _split3(x):
    hi = x.astype(BF16)
    r1 = x - hi.astype(F32)
    mid = r1.astype(BF16)
    lo = (r1 - mid.astype(F32)).astype(BF16)
    return hi, mid, lo


def _rms(x, g):
    return x * lax.rsqrt(jnp.mean(x * x, axis=-1, keepdims=True) + EPS) * g


def _rmsnorm_kernel(x_ref, g_ref, o_ref):
    o_ref[...] = _rms(x_ref[...], g_ref[...]).astype(o_ref.dtype)


def rmsnorm(x, g, out_dtype, tm=512):
    t, d = x.shape
    tm = min(tm, t)
    return pl.pallas_call(
        _rmsnorm_kernel, grid=(t // tm,),
        in_specs=[pl.BlockSpec((tm, d), lambda i: (i, 0)), pl.BlockSpec((1, d), lambda i: (0, 0))],
        out_specs=pl.BlockSpec((tm, d), lambda i: (i, 0)),
        out_shape=jax.ShapeDtypeStruct((t, d), out_dtype),
        compiler_params=_params("parallel"), name="rmsnorm",
    )(x, g.reshape(1, d))


META_E1, META_E2, META_RANK1, META_RANK2, META_W1, META_W2 = range(6)


def _lane_col(x, lane, idx):
    return jnp.sum(jnp.where(lane == idx, x, 0.0), axis=1, keepdims=True)


def _rmsnorm_router_kernel(x_ref, g_ref, wr_ref, u3_ref, meta_ref, counts_ref, carry_ref, *, n_experts):
    i = pl.program_id(0)
    tm = x_ref.shape[0]

    @pl.when(i == 0)
    def _():
        carry_ref[...] = jnp.zeros_like(carry_ref)

    xn = _rms(x_ref[...], g_ref[...])
    u3_ref[...] = xn.astype(u3_ref.dtype).reshape(u3_ref.shape)
    xh, xm, _ = _split3(xn)
    wh, wm, _ = _split3(wr_ref[...])
    logits = _dot(xh, wh) + _dot(xm, wh) + _dot(xh, wm)
    lane = lax.broadcasted_iota(jnp.int32, logits.shape, 1).astype(F32)
    neg = jnp.float32(-jnp.inf)
    l1 = jnp.where(lane < n_experts, logits, neg)
    m1 = jnp.max(l1, axis=-1, keepdims=True)
    i1 = jnp.min(jnp.where(l1 == m1, lane, float(LANES)), axis=-1, keepdims=True)
    l2 = jnp.where(lane == i1, neg, l1)
    m2 = jnp.max(l2, axis=-1, keepdims=True)
    i2 = jnp.min(jnp.where(l2 == m2, lane, float(LANES)), axis=-1, keepdims=True)
    e2 = jnp.exp(m2 - m1)
    w1 = 1.0 / (1.0 + e2)
    w2 = e2 / (1.0 + e2)
    onehot = jnp.where(lane == i1, 1.0, 0.0) + jnp.where(lane == i2, 1.0, 0.0)
    row = lax.broadcasted_iota(jnp.int32, (tm, tm), 0)
    col = lax.broadcasted_iota(jnp.int32, (tm, tm), 1)
    before = jnp.where(row > col, 1.0, 0.0).astype(BF16)
    excl = _dot(before, onehot.astype(BF16)) + carry_ref[...]
    rank1 = _lane_col(excl, lane, i1)
    rank2 = _lane_col(excl, lane, i2)
    total = carry_ref[...] + jnp.sum(onehot, axis=0, keepdims=True)
    carry_ref[...] = total
    counts_ref[...] = total
    meta = jnp.zeros_like(logits)
    for idx, val in ((META_E1, i1), (META_E2, i2), (META_RANK1, rank1), (META_RANK2, rank2),
                     (META_W1, w1), (META_W2, w2)):
        meta = jnp.where(lane == idx, val, meta)
    meta_ref[...] = meta


def rmsnorm_router(x, g, w_router, tm=256):
    t, d = x.shape
    tm = min(tm, t)
    n_experts = w_router.shape[1]
    wr = jnp.pad(w_router, ((0, 0), (0, LANES - n_experts)))
    return pl.pallas_call(
        functools.partial(_rmsnorm_router_kernel, n_experts=n_experts), grid=(t // tm,),
        in_specs=[pl.BlockSpec((tm, d), lambda i: (i, 0)), pl.BlockSpec((1, d), lambda i: (0, 0)),
                  pl.BlockSpec((d, LANES), lambda i: (0, 0))],
        out_specs=[pl.BlockSpec((tm, d // LANES, LANES), lambda i: (i, 0, 0)),
                   pl.BlockSpec((tm, LANES), lambda i: (i, 0)),
                   pl.BlockSpec((1, LANES), lambda i: (0, 0))],
        out_shape=[jax.ShapeDtypeStruct((t, d // LANES, LANES), BF16), jax.ShapeDtypeStruct((t, LANES), F32),
                   jax.ShapeDtypeStruct((1, LANES), F32)],
        scratch_shapes=[pltpu.VMEM((1, LANES), F32)],
        compiler_params=_params("arbitrary"), name="rmsnorm_router",
    )(x, g.reshape(1, d), wr)


def _mm_kernel(a_ref, b_ref, o_ref):
    o_ref[...] = _dot(a_ref[...], b_ref[...]).astype(o_ref.dtype)


def matmul(a, b, out_dtype, tm, tn):
    m, k = a.shape
    n = b.shape[1]
    tm, tn = min(tm, m), min(tn, n)
    return pl.pallas_call(
        _mm_kernel, grid=(m // tm, n // tn),
        in_specs=[pl.BlockSpec((tm, k), lambda i, j: (i, 0)), pl.BlockSpec((k, tn), lambda i, j: (0, j))],
        out_specs=pl.BlockSpec((tm, tn), lambda i, j: (i, j)),
        out_shape=jax.ShapeDtypeStruct((m, n), out_dtype),
        compiler_params=_params("parallel", "parallel"), name="matmul",
    )(a, b)


def _mm_res_kernel(a_ref, b_ref, r_ref, o_ref):
    o_ref[...] = r_ref[...] + _dot(a_ref[...], b_ref[...])


def matmul_residual(a, b, res, tm, tn, a_buffers=2):
    m, k = a.shape
    n = b.shape[1]
    tm, tn = min(tm, m), min(tn, n)
    a_spec = pl.BlockSpec((tm, k), lambda i, j: (i, 0), pipeline_mode=pl.Buffered(a_buffers))
    return pl.pallas_call(
        _mm_res_kernel, grid=(m // tm, n // tn),
        in_specs=[a_spec, pl.BlockSpec((k, tn), lambda i, j: (0, j)),
                  pl.BlockSpec((tm, tn), lambda i, j: (i, j))],
        out_specs=pl.BlockSpec((tm, tn), lambda i, j: (i, j)),
        out_shape=jax.ShapeDtypeStruct((m, n), F32),
        compiler_params=_params("parallel", "parallel"), name="matmul_residual",
    )(a, b, res)


def _swiglu_kernel(a_ref, wg_ref, wu_ref, o_ref):
    a = a_ref[...]
    g = _dot(a, wg_ref[...])
    u = _dot(a, wu_ref[...])
    o_ref[...] = (_silu(g) * u).astype(o_ref.dtype)


def swiglu_up(a, w_gate, w_up, tm, tn):
    m, k = a.shape
    n = w_gate.shape[1]
    tm, tn = min(tm, m), min(tn, n)
    return pl.pallas_call(
        _swiglu_kernel, grid=(m // tm, n // tn),
        in_specs=[pl.BlockSpec((tm, k), lambda i, j: (i, 0)), pl.BlockSpec((k, tn), lambda i, j: (0, j)),
                  pl.BlockSpec((k, tn), lambda i, j: (0, j))],
        out_specs=pl.BlockSpec((tm, tn), lambda i, j: (i, j)),
        out_shape=jax.ShapeDtypeStruct((m, n), BF16),
        compiler_params=_params("parallel", "parallel"), name="swiglu_up",
    )(a, w_gate, w_up)


def _moe_dispatch_kernel(s1_ref, s2_ref, u3_ref, xs_in_ref, xs_ref, sem, *, r):
    del xs_in_ref
    base = pl.program_id(0) * r

    def copies(k):
        t = base + k
        return [pltpu.make_async_copy(u3_ref.at[t], xs_ref.at[s1_ref[t]], sem),
                pltpu.make_async_copy(u3_ref.at[t], xs_ref.at[s2_ref[t]], sem)]

    def start(k, c):
        for cp in copies(k):
            cp.start()
        return c

    def wait(k, c):
        for cp in copies(k):
            cp.wait()
        return c

    lax.fori_loop(0, r, start, 0)
    lax.fori_loop(0, r, wait, 0)


def moe_dispatch(u3, slot1, slot2, n_slots, r=128):
    t, s, l = u3.shape
    r = min(r, t)
    return pl.pallas_call(
        functools.partial(_moe_dispatch_kernel, r=r),
        grid_spec=pltpu.PrefetchScalarGridSpec(
            num_scalar_prefetch=2, grid=(t // r,),
            in_specs=[pl.BlockSpec(memory_space=pl.ANY), pl.BlockSpec(memory_space=pl.ANY)],
            out_specs=pl.BlockSpec(memory_space=pl.ANY),
            scratch_shapes=[pltpu.SemaphoreType.DMA(())]),
        out_shape=jax.ShapeDtypeStruct((n_slots, s, l), u3.dtype),
        input_output_aliases={3: 0},
        compiler_params=_params("arbitrary"), name="moe_dispatch",
    )(slot1, slot2, u3, jnp.zeros((n_slots, s, l), u3.dtype))


def _moe_up_kernel(te_ref, nu_ref, a3_ref, wg_ref, wu_ref, o_ref, a2_ref):
    i = pl.program_id(0)
    j = pl.program_id(1)

    @pl.when(j == 0)
    def _():
        a2_ref[...] = a3_ref[...].reshape(a2_ref.shape)

    @pl.when(i < nu_ref[0])
    def _():
        a = a2_ref[...]
        g = _dot(a, wg_ref[...])
        u = _dot(a, wu_ref[...])
        o_ref[...] = (_silu(g) * u).astype(o_ref.dtype)

    @pl.when(i >= nu_ref[0])
    def _():
        o_ref[...] = jnp.zeros_like(o_ref)


def moe_up(xs3, tile_expert, n_used, w_gate, w_up, tmg, tn):
    p, s, l = xs3.shape
    n_e, k, f = w_gate.shape
    tn = min(tn, f)
    wspec = pl.BlockSpec((None, k, tn), lambda i, j, te, nu: (te[i], 0, j))
    return pl.pallas_call(
        _moe_up_kernel,
        grid_spec=pltpu.PrefetchScalarGridSpec(
            num_scalar_prefetch=2, grid=(p // tmg, f // tn),
            in_specs=[pl.BlockSpec((tmg, s, l), lambda i, j, te, nu: (i, 0, 0)), wspec, wspec],
            out_specs=pl.BlockSpec((tmg, tn), lambda i, j, te, nu: (i, j)),
            scratch_shapes=[pltpu.VMEM((tmg, k), xs3.dtype)]),
        out_shape=jax.ShapeDtypeStruct((p, f), BF16),
        compiler_params=_params("arbitrary", "arbitrary"), name="moe_up",
    )(tile_expert, n_used, xs3, w_gate, w_up)


def _moe_down_kernel(te_ref, nu_ref, a_ref, wd_ref, o3_ref):
    i = pl.program_id(0)

    @pl.when(i < nu_ref[0])
    def _():
        o3_ref[...] = _dot(a_ref[...], wd_ref[...]).astype(o3_ref.dtype).reshape(o3_ref.shape)

    @pl.when(i >= nu_ref[0])
    def _():
        o3_ref[...] = jnp.zeros_like(o3_ref)


def moe_down(hid, tile_expert, n_used, w_down, tmg):
    p, f = hid.shape
    n_e, _, d = w_down.shape
    return pl.pallas_call(
        _moe_down_kernel,
        grid_spec=pltpu.PrefetchScalarGridSpec(
            num_scalar_prefetch=2, grid=(p // tmg,),
            in_specs=[pl.BlockSpec((tmg, f), lambda i, te, nu: (i, 0)),
                      pl.BlockSpec((None, f, d), lambda i, te, nu: (te[i], 0, 0))],
            out_specs=pl.BlockSpec((tmg, d // LANES, LANES), lambda i, te, nu: (i, 0, 0))),
        out_shape=jax.ShapeDtypeStruct((p, d // LANES, LANES), BF16),
        compiler_params=_params("arbitrary"), name="moe_down",
    )(tile_expert, n_used, hid, w_down)


def _moe_combine_kernel(s1_ref, s2_ref, h_ref, meta_ref, y3_ref, o_ref, ya_ref, yb_ref, sem, *, r):
    base = pl.program_id(0) * r

    def copies(k):
        t = base + k
        return [pltpu.make_async_copy(y3_ref.at[s1_ref[t]], ya_ref.at[k], sem),
                pltpu.make_async_copy(y3_ref.at[s2_ref[t]], yb_ref.at[k], sem)]

    def start(k, c):
        for cp in copies(k):
            cp.start()
        return c

    def wait(k, c):
        for cp in copies(k):
            cp.wait()
        return c

    lax.fori_loop(0, r, start, 0)
    meta = meta_ref[...]
    lane = lax.broadcasted_iota(jnp.int32, meta.shape, 1)
    w1 = _lane_col(meta, lane, META_W1)
    w2 = _lane_col(meta, lane, META_W2)
    lax.fori_loop(0, r, wait, 0)
    d = h_ref.shape[1]
    ya = ya_ref[...].reshape(r, d).astype(F32)
    yb = yb_ref[...].reshape(r, d).astype(F32)
    o_ref[...] = h_ref[...] + w1 * ya + w2 * yb


def moe_combine(h, meta, y3, slot1, slot2, r=256):
    t, d = h.shape
    _, s, l = y3.shape
    r = min(r, t)
    return pl.pallas_call(
        functools.partial(_moe_combine_kernel, r=r),
        grid_spec=pltpu.PrefetchScalarGridSpec(
            num_scalar_prefetch=2, grid=(t // r,),
            in_specs=[pl.BlockSpec((r, d), lambda i, s1, s2: (i, 0)),
                      pl.BlockSpec((r, LANES), lambda i, s1, s2: (i, 0)),
                      pl.BlockSpec(memory_space=pl.ANY)],
            out_specs=pl.BlockSpec((r, d), lambda i, s1, s2: (i, 0)),
            scratch_shapes=[pltpu.VMEM((r, s, l), y3.dtype), pltpu.VMEM((r, s, l), y3.dtype),
                            pltpu.SemaphoreType.DMA(())]),
        out_shape=jax.ShapeDtypeStruct((t, d), F32),
        compiler_params=_params("arbitrary"), name="moe_combine",
    )(slot1, slot2, h, meta, y3)


def _moe_plan(meta, counts, n_experts, tmg, n_tiles):
    i32 = jnp.int32
    e1, e2 = meta[:, META_E1].astype(i32), meta[:, META_E2].astype(i32)
    r1, r2 = meta[:, META_RANK1].astype(i32), meta[:, META_RANK2].astype(i32)
    cnt = counts[0, :n_experts].astype(i32)
    padded = (cnt + tmg - 1) // tmg * tmg
    ends = jnp.cumsum(padded)
    starts = ends - padded
    slot1 = starts[e1] + r1
    slot2 = starts[e2] + r2
    tile_row0 = jnp.arange(n_tiles, dtype=i32) * tmg
    tile_expert = jnp.minimum(jnp.sum(ends[None, :] <= tile_row0[:, None], axis=1), n_experts - 1).astype(i32)
    n_used = (ends[-1:] // tmg).astype(i32)
    return slot1, slot2, tile_expert, n_used


def _softplus(x):
    return jnp.maximum(x, 0.0) + jnp.log1p(jnp.exp(-jnp.abs(x)))


def _conv_silu_kernel(x_ref, halo_ref, w_ref, b_ref, o_ref, buf_ref, *, seq_tiles, k_conv):
    i = pl.program_id(0)
    ts = x_ref.shape[0]
    hl = halo_ref.shape[0]
    x = x_ref[...].astype(F32)
    halo = halo_ref[...].astype(F32)
    buf_ref[0:hl, :] = jnp.where(i % seq_tiles == 0, 0.0, halo)
    buf_ref[hl:hl + ts, :] = x
    acc = x * w_ref[k_conv - 1:k_conv, :] + b_ref[...]
    for s in range(1, k_conv):
        acc = acc + buf_ref[hl - s:hl - s + ts, :] * w_ref[k_conv - 1 - s:k_conv - s, :]
    o_ref[...] = _silu(acc).astype(o_ref.dtype)


def conv_silu(zx, col0, n_cols, conv_w, conv_b, seq, ts=512, tc=1024, hl=16):
    t = zx.shape[0]
    k_conv = conv_w.shape[0]
    ts, tc = min(ts, seq), min(tc, n_cols)
    cb0 = col0 // tc
    return pl.pallas_call(
        functools.partial(_conv_silu_kernel, seq_tiles=seq // ts, k_conv=k_conv),
        grid=(t // ts, n_cols // tc),
        in_specs=[pl.BlockSpec((ts, tc), lambda i, j: (i, cb0 + j)),
                  pl.BlockSpec((hl, tc), lambda i, j: (jnp.maximum(i * (ts // hl) - 1, 0), cb0 + j)),
                  pl.BlockSpec((k_conv, tc), lambda i, j: (0, j)),
                  pl.BlockSpec((1, tc), lambda i, j: (0, j))],
        out_specs=pl.BlockSpec((ts, tc), lambda i, j: (i, j)),
        out_shape=jax.ShapeDtypeStruct((t, n_cols), BF16),
        scratch_shapes=[pltpu.VMEM((hl + ts, tc), F32)],
        compiler_params=_params("parallel", "parallel"), name="conv_silu",
    )(zx, zx, conv_w, conv_b.reshape(1, n_cols))


def _ssd_scan_kernel(x_ref, b_ref, c_ref, z_ref, dt_ref, dtb_ref, alog_ref, dsk_ref, ng_ref, e_ref,
                     o_ref, state_ref, acumT_ref, dtT_ref, y_ref, *, hg):
    g = pl.program_id(1)
    c = pl.program_id(2)
    q = x_ref.shape[0]

    @pl.when(c == 0)
    def _():
        state_ref[...] = jnp.zeros_like(state_ref)

    dt = _softplus(dt_ref[...] + dtb_ref[...])
    a = dt * (-jnp.exp(alog_ref[...]))
    row = lax.broadcasted_iota(jnp.int32, (q, q), 0)
    col = lax.broadcasted_iota(jnp.int32, (q, q), 1)
    causal = row >= col
    tri = jnp.where(causal, 1.0, 0.0).astype(BF16)
    a_hi, a_mid, a_lo = _split3(a)
    acum = _dot(tri, a_hi) + _dot(tri, a_mid) + _dot(tri, a_lo)
    acumT_ref[...] = acum.T
    dtT_ref[...] = dt.T
    a_last = acum[q - 1:q, :]
    w_in = jnp.exp(a_last - acum) * dt
    e_a = jnp.exp(acum)
    expd = _dot(jnp.concatenate([w_in, e_a], axis=0).astype(BF16), e_ref[...])
    w_exp = expd[:q, :]
    e_exp = expd[q:, :]

    bm = b_ref[...]
    cm = c_ref[...]
    cb = _dot_nt(cm, bm)
    lane = lax.broadcasted_iota(jnp.int32, (q, LANES), 1)
    heads_per_tile = LANES // SSD_HEAD_DIM
    for p in range(hg // heads_per_tile):
        ms = []
        for hh in range(heads_per_tile):
            hidx = g * hg + p * heads_per_tile + hh
            colv = jnp.sum(jnp.where(lane == hidx, acum, 0.0), axis=1, keepdims=True)
            rowv = acumT_ref[pl.ds(hidx, 1), :]
            dtr = dtT_ref[pl.ds(hidx, 1), :]
            decay = jnp.where(causal, jnp.exp(jnp.minimum(colv - rowv, 0.0)), 0.0)
            ms.append((cb * decay * dtr).astype(BF16))
        m_cat = jnp.concatenate(ms, axis=1)
        xp = x_ref[:, p * LANES:(p + 1) * LANES].astype(F32)
        x_bd = jnp.concatenate(
            [jnp.where((lane // SSD_HEAD_DIM) == hh, xp, 0.0).astype(BF16) for hh in range(heads_per_tile)],
            axis=0)
        y_ref[:, p * LANES:(p + 1) * LANES] = _dot(m_cat, x_bd)

    st = state_ref[...]
    xf = x_ref[...].astype(F32)
    y_off = _dot(cm, st.astype(BF16)) * e_exp
    xw = (xf * w_exp).astype(BF16)
    bt = bm.astype(F32).T.astype(BF16)
    state_ref[...] = st * e_exp[q - 1:q, :] + _dot(bt, xw)
    y = y_ref[...] + y_off + xf * dsk_ref[...]
    y = y * _silu(z_ref[...].astype(F32))
    y = y * lax.rsqrt(jnp.mean(y * y, axis=-1, keepdims=True) + EPS) * ng_ref[...]
    o_ref[...] = y.astype(o_ref.dtype)


def ssd_scan(xbc, zx, dt_raw, dt_bias, a_log, d_skip, norm_g, batch, seq):
    t = xbc.shape[0]
    heads = dt_raw.shape[1]
    assert heads == LANES
    hg = heads // SSD_GROUPS
    gw = hg * SSD_HEAD_DIM
    d_inner = heads * SSD_HEAD_DIM
    q = SSD_CHUNK
    nc = seq // q
    n = SSD_STATE
    assert gw % LANES == 0 and n == LANES
    b_blk0 = d_inner // n
    c_blk0 = (d_inner + SSD_GROUPS * n) // n
    ch = jnp.arange(gw) // SSD_HEAD_DIM
    expand = (jnp.arange(heads)[None, :, None] ==
              (jnp.arange(SSD_GROUPS)[:, None, None] * hg + ch[None, None, :])).astype(BF16)
    d_exp = jnp.repeat(d_skip, SSD_HEAD_DIM).reshape(1, d_inner)
    rowmap = lambda b, g, c: (b * nc + c)
    return pl.pallas_call(
        functools.partial(_ssd_scan_kernel, hg=hg),
        grid=(batch, SSD_GROUPS, nc),
        in_specs=[pl.BlockSpec((q, gw), lambda b, g, c: (rowmap(b, g, c), g)),
                  pl.BlockSpec((q, n), lambda b, g, c: (rowmap(b, g, c), b_blk0 + g)),
                  pl.BlockSpec((q, n), lambda b, g, c: (rowmap(b, g, c), c_blk0 + g)),
                  pl.BlockSpec((q, gw), lambda b, g, c: (rowmap(b, g, c), g)),
                  pl.BlockSpec((q, heads), lambda b, g, c: (rowmap(b, g, c), 0)),
                  pl.BlockSpec((1, heads), lambda b, g, c: (0, 0)),
                  pl.BlockSpec((1, heads), lambda b, g, c: (0, 0)),
                  pl.BlockSpec((1, gw), lambda b, g, c: (0, g)),
                  pl.BlockSpec((1, gw), lambda b, g, c: (0, g)),
                  pl.BlockSpec((None, heads, gw), lambda b, g, c: (g, 0, 0))],
        out_specs=pl.BlockSpec((q, gw), lambda b, g, c: (rowmap(b, g, c), g)),
        out_shape=jax.ShapeDtypeStruct((t, d_inner), BF16),
        scratch_shapes=[pltpu.VMEM((n, gw), F32), pltpu.VMEM((heads, q), F32), pltpu.VMEM((heads, q), F32),
                        pltpu.VMEM((q, gw), F32)],
        compiler_params=_params("parallel", "parallel", "arbitrary"), name="ssd_scan",
    )(xbc, xbc, xbc, zx, dt_raw, dt_bias.reshape(1, heads), a_log.reshape(1, heads), d_exp,
      norm_g.reshape(1, d_inner), expand)


def _shortconv_in_kernel(a_ref, wb_ref, wc_ref, wh_ref, cw_ref, o_ref, carry_ref, buf_ref, *, seq_tiles, k_conv):
    i = pl.program_id(0)
    j = pl.program_id(1)
    tm = a_ref.shape[0]
    hl = carry_ref.shape[1]
    a = a_ref[...]
    gate_b = _dot(a, wb_ref[...])
    v = _dot(a, wc_ref[...]) * _dot(a, wh_ref[...])
    @pl.when(i % seq_tiles == 0)
    def _():
        buf_ref[0:hl, :] = jnp.zeros((hl, v.shape[1]), F32)

    @pl.when(i % seq_tiles != 0)
    def _():
        buf_ref[0:hl, :] = carry_ref[j]

    buf_ref[hl:hl + tm, :] = v
    carry_ref[j] = v[tm - hl:, :]
    acc = v * cw_ref[k_conv - 1:k_conv, :]
    for s in range(1, k_conv):
        acc = acc + buf_ref[hl - s:hl - s + tm, :] * cw_ref[k_conv - 1 - s:k_conv - s, :]
    o_ref[...] = (gate_b * acc).astype(o_ref.dtype)


def shortconv_in(a, w_b, w_c, w_h, conv_w, seq, tm=1024, tn=256, hl=8):
    m, k = a.shape
    n = w_b.shape[1]
    k_conv = conv_w.shape[0]
    tm, tn = min(tm, seq), min(tn, n)
    wspec = pl.BlockSpec((k, tn), lambda i, j: (0, j))
    return pl.pallas_call(
        functools.partial(_shortconv_in_kernel, seq_tiles=seq // tm, k_conv=k_conv),
        grid=(m // tm, n // tn),
        in_specs=[pl.BlockSpec((tm, k), lambda i, j: (i, 0)), wspec, wspec, wspec,
                  pl.BlockSpec((k_conv, tn), lambda i, j: (0, j))],
        out_specs=pl.BlockSpec((tm, tn), lambda i, j: (i, j)),
        out_shape=jax.ShapeDtypeStruct((m, n), BF16),
        scratch_shapes=[pltpu.VMEM((n // tn, hl, tn), F32), pltpu.VMEM((hl + tm, tn), F32)],
        compiler_params=_params("arbitrary", "arbitrary"), name="shortconv_in",
    )(a, w_b, w_c, w_h, conv_w)


def _pool_kernel(h_ref, halo_ref, g_ref, w_ref, sc_ref, o_ref, buf_ref, *, seq, windows):
    i = pl.program_id(0)
    tm, d = h_ref.shape
    hl = halo_ref.shape[0]
    gw = d // len(windows)
    h = h_ref[...]
    g = g_ref[...]
    xn = _rms(h, g)
    row0 = i * tm
    seq_start = (row0 % seq) == 0
    buf_ref[0:hl, :] = jnp.where(seq_start, 0.0, _rms(halo_ref[...], g))
    buf_ref[hl:hl + tm, :] = xn
    pos = (row0 + lax.broadcasted_iota(jnp.int32, (tm, 1), 0)) % seq
    for gi, w in enumerate(windows):
        cs = slice(gi * gw, (gi + 1) * gw)
        tok = xn[:, cs]
        acc = tok
        for s in range(1, w):
            acc = acc + buf_ref[hl - s:hl - s + tm, cs]
        cnt = jnp.minimum(pos + 1, w).astype(F32)
        pooled = (acc / cnt - tok).astype(BF16)
        y = _dot(pooled, w_ref[gi])
        o_ref[:, cs] = h[:, cs] + y * sc_ref[:, cs]


def pool_mixer_residual(h, norm_g, w_group, scale, seq, tm=256):
    t, d = h.shape
    tm = min(tm, seq)
    hl = POOL_HALO
    n_g, gw, _ = w_group.shape
    return pl.pallas_call(
        functools.partial(_pool_kernel, seq=seq, windows=POOL_WINDOWS),
        grid=(t // tm,),
        in_specs=[pl.BlockSpec((tm, d), lambda i: (i, 0)),
                  pl.BlockSpec((hl, d), lambda i: (jnp.maximum(i * (tm // hl) - 1, 0), 0)),
                  pl.BlockSpec((1, d), lambda i: (0, 0)),
                  pl.BlockSpec((n_g, gw, gw), lambda i: (0, 0, 0)),
                  pl.BlockSpec((1, d), lambda i: (0, 0))],
        out_specs=pl.BlockSpec((tm, d), lambda i: (i, 0)),
        out_shape=jax.ShapeDtypeStruct((t, d), F32),
        scratch_shapes=[pltpu.VMEM((hl + tm, d), F32)],
        compiler_params=_params("parallel"), name="pool_mixer",
    )(h, h, norm_g.reshape(1, d), w_group, scale.reshape(1, d))


def _retention_kernel(q_ref, k_ref, v_ref, g_ref, cos_ref, sin_ref, inner_ref, qd_ref, kd_ref, cd_ref,
                      o_ref, state_ref):
    c = pl.program_id(2)

    @pl.when(c == 0)
    def _():
        state_ref[...] = jnp.zeros_like(state_ref)

    dk = q_ref.shape[1]
    half = dk // 2
    cos = cos_ref[...]
    sin = sin_ref[...]

    def rot(t):
        t1, t2 = t[:, :half], t[:, half:]
        return jnp.concatenate([t1 * cos - t2 * sin, t1 * sin + t2 * cos], axis=1)

    qr = rot(q_ref[...].astype(F32))
    kr = rot(k_ref[...].astype(F32)) * (dk ** -0.5)
    v = v_ref[...]
    scores = _dot_nt(qr.astype(BF16), kr.astype(BF16)) * inner_ref[...]
    st = state_ref[...]
    y = _dot(scores.astype(BF16), v) + _dot((qr * qd_ref[...]).astype(BF16), st.astype(BF16))
    kdt = (kr * kd_ref[...]).T.astype(BF16)
    state_ref[...] = st * cd_ref[...] + _dot(kdt, v)
    y = y * lax.rsqrt(jnp.mean(y * y, axis=-1, keepdims=True) + EPS)
    o_ref[...] = (y * _silu(g_ref[...].astype(F32))).astype(o_ref.dtype)


def retention(qkvg, batch, seq):
    t, n_in = qkvg.shape
    hh = RET_HEADS
    dk = n_in // (6 * hh)
    dv = 2 * dk
    q = RET_CHUNK
    nc = seq // q
    half = dk // 2
    pos = jnp.arange(seq, dtype=F32)
    inv = ROPE_BASE ** (-jnp.arange(half, dtype=F32) / half)
    ang = pos[:, None] * inv[None]
    cos, sin = jnp.cos(ang), jnp.sin(ang)
    log_gamma = jnp.log1p(-jnp.exp2(-5.0 - jnp.arange(hh, dtype=F32)))
    idx = jnp.arange(q, dtype=F32)
    rel = idx[:, None] - idx[None, :]
    inner = jnp.exp(jnp.where(rel[None] >= 0, rel[None] * log_gamma[:, None, None], -jnp.inf))
    q_decay = jnp.exp((idx + 1.0)[None, :] * log_gamma[:, None])[..., None]
    k_decay = jnp.exp((q - 1.0 - idx)[None, :] * log_gamma[:, None])[..., None]
    c_decay = jnp.exp(q * log_gamma)[:, None, None]
    rowmap = lambda b, h, c: b * nc + c
    return pl.pallas_call(
        _retention_kernel, grid=(batch, hh, nc),
        in_specs=[pl.BlockSpec((q, dk), lambda b, h, c: (rowmap(b, h, c), h)),
                  pl.BlockSpec((q, dk), lambda b, h, c: (rowmap(b, h, c), hh + h)),
                  pl.BlockSpec((q, dv), lambda b, h, c: (rowmap(b, h, c), hh + h)),
                  pl.BlockSpec((q, dv), lambda b, h, c: (rowmap(b, h, c), 2 * hh + h)),
                  pl.BlockSpec((q, half), lambda b, h, c: (c, 0)),
                  pl.BlockSpec((q, half), lambda b, h, c: (c, 0)),
                  pl.BlockSpec((None, q, q), lambda b, h, c: (h, 0, 0)),
                  pl.BlockSpec((None, q, 1), lambda b, h, c: (h, 0, 0)),
                  pl.BlockSpec((None, q, 1), lambda b, h, c: (h, 0, 0)),
                  pl.BlockSpec((None, 1, 1), lambda b, h, c: (h, 0, 0))],
        out_specs=pl.BlockSpec((q, dv), lambda b, h, c: (rowmap(b, h, c), h)),
        out_shape=jax.ShapeDtypeStruct((t, hh * dv), BF16),
        scratch_shapes=[pltpu.VMEM((dk, dv), F32)],
        compiler_params=_params("parallel", "parallel", "arbitrary"), name="retention",
    )(qkvg, qkvg, qkvg, qkvg, cos, sin, inner, q_decay, k_decay, c_decay)


def _dense_ffn(h, norm_g, w_gate, w_up, w_down):
    u = rmsnorm(h, norm_g, BF16)
    hid = swiglu_up(u, w_gate.astype(BF16), w_up.astype(BF16), tm=2048, tn=256)
    return matmul_residual(hid, w_down.astype(BF16), h, tm=1024, tn=256, a_buffers=1)


def _moe_ffn(h, norm_g, w_router, w_gate, w_up, w_down, tmg=512):
    t, d = h.shape
    n_e = w_router.shape[1]
    tmg = min(tmg, t)
    n_tiles = (2 * t) // tmg + n_e
    u3, meta, counts = rmsnorm_router(h, norm_g, w_router)
    slot1, slot2, tile_expert, n_used = _moe_plan(meta, counts, n_e, tmg, n_tiles)
    xs3 = moe_dispatch(u3, slot1, slot2, n_tiles * tmg)
    hid = moe_up(xs3, tile_expert, n_used, w_gate.astype(BF16), w_up.astype(BF16), tmg, tn=768)
    y3 = moe_down(hid, tile_expert, n_used, w_down.astype(BF16), tmg)
    return moe_combine(h, meta, y3, slot1, slot2)


def kernel(x, norm_mix, norm_ffn, norm_final, ssd_w_in, ssd_conv_w, ssd_conv_b, ssd_dt_bias, ssd_a_log, ssd_d, ssd_norm, ssd_w_out, sc_w_in, sc_conv_w, sc_w_out, pool_w, pool_scale, ret_w_qkvg, ret_w_out, ffn0_w_gate, ffn0_w_up, ffn0_w_down, moe1_router, moe1_w_gate, moe1_w_up, moe1_w_down, ffn2_w_gate, ffn2_w_up, ffn2_w_down, moe3_router, moe3_w_gate, moe3_w_up, moe3_w_down):
    batch, seq, d = x.shape
    t = batch * seq
    h = x.reshape(t, d)

    heads = ssd_dt_bias.shape[0]
    d_inner = heads * SSD_HEAD_DIM
    n_xbc = ssd_conv_w.shape[1]
    u = rmsnorm(h, norm_mix[0], BF16)
    zx = matmul(u, ssd_w_in[:, :d_inner + n_xbc].astype(BF16), BF16, tm=1024, tn=1024)
    dt_raw = matmul(u, ssd_w_in[:, d_inner + n_xbc:].astype(BF16), F32, tm=1024, tn=LANES)
    xbc = conv_silu(zx, d_inner, n_xbc, ssd_conv_w, ssd_conv_b, seq)
    y = ssd_scan(xbc, zx, dt_raw, ssd_dt_bias, ssd_a_log, ssd_d, ssd_norm, batch, seq)
    h = matmul_residual(y, ssd_w_out.astype(BF16), h, tm=1024, tn=256)
    h = _dense_ffn(h, norm_ffn[0], ffn0_w_gate, ffn0_w_up, ffn0_w_down)

    u = rmsnorm(h, norm_mix[1], BF16)
    w_in = sc_w_in.astype(BF16)
    y = shortconv_in(u, w_in[:, :d], w_in[:, d:2 * d], w_in[:, 2 * d:], sc_conv_w, seq)
    h = matmul_residual(y, sc_w_out.astype(BF16), h, tm=1024, tn=512)
    h = _moe_ffn(h, norm_ffn[1], moe1_router, moe1_w_gate, moe1_w_up, moe1_w_down)

    h = pool_mixer_residual(h, norm_mix[2], pool_w.astype(BF16), pool_scale, seq)
    h = _dense_ffn(h, norm_ffn[2], ffn2_w_gate, ffn2_w_up, ffn2_w_down)

    u = rmsnorm(h, norm_mix[3], BF16)
    qkvg = matmul(u, ret_w_qkvg.astype(BF16), BF16, tm=1024, tn=1024)
    y = retention(qkvg, batch, seq)
    h = matmul_residual(y, ret_w_out.astype(BF16), h, tm=1024, tn=256)
    h = _moe_ffn(h, norm_ffn[3], moe3_router, moe3_w_gate, moe3_w_up, moe3_w_down)

    return rmsnorm(h, norm_final, F32).reshape(batch, seq, d)
```

```python
import functools

import jax
import jax.numpy as jnp
from jax import lax
from jax.experimental import pallas as pl
from jax.experimental.pallas import tpu as pltpu

F32 = jnp.float32
BF16 = jnp.bfloat16
EPS = 1e-6
V7X_VMEM_LIMIT_BYTES = 60 * 1024 * 1024
LANES = 128

SSD_HEAD_DIM = 64
SSD_GROUPS = 8
SSD_STATE = 128
SSD_CHUNK = 128
POOL_WINDOWS = (2, 4, 8, 16)
POOL_HALO = 16
RET_HEADS = 16
RET_CHUNK = 128
ROPE_BASE = 10000.0
N_EXPERTS = 8


def _params(*sem):
    return pltpu.CompilerParams(dimension_semantics=sem, vmem_limit_bytes=V7X_VMEM_LIMIT_BYTES)


def _silu(x):
    return x / (1.0 + jnp.exp(-x))


def _dot(a, b):
    return jnp.dot(a, b, preferred_element_type=F32)


def _dot_nt(a, b):
    return lax.dot_general(a, b, (((1,), (1,)), ((), ())), preferred_element_type=F32)


def _split3(x):
    hi = x.astype(BF16)
    r1 = x - hi.astype(F32)
    mid = r1.astype(BF16)
    lo = (r1 - mid.astype(F32)).astype(BF16)
    return hi, mid, lo


def _rms(x, g):
    return x * lax.rsqrt(jnp.mean(x * x, axis=-1, keepdims=True) + EPS) * g


def _rmsnorm_kernel(x_ref, g_ref, o_ref):
    o_ref[...] = _rms(x_ref[...], g_ref[...]).astype(o_ref.dtype)


def rmsnorm(x, g, out_dtype, tm=512):
    t, d = x.shape
    tm = min(tm, t)
    return pl.pallas_call(
        _rmsnorm_kernel, grid=(t // tm,),
        in_specs=[pl.BlockSpec((tm, d), lambda i: (i, 0)), pl.BlockSpec((1, d), lambda i: (0, 0))],
        out_specs=pl.BlockSpec((tm, d), lambda i: (i, 0)),
        out_shape=jax.ShapeDtypeStruct((t, d), out_dtype),
        compiler_params=_params("parallel"), name="rmsnorm",
    )(x, g.reshape(1, d))


META_E1, META_E2, META_RANK1, META_RANK2, META_W1, META_W2 = range(6)


def _lane_col(x, lane, idx):
    return jnp.sum(jnp.where(lane == idx, x, 0.0), axis=1, keepdims=True)


def _rmsnorm_router_kernel(x_ref, g_ref, wr_ref, u3_ref, meta_ref, counts_ref, carry_ref, *, n_experts):
    i = pl.program_id(0)
    tm = x_ref.shape[0]

    @pl.when(i == 0)
    def _():
        carry_ref[...] = jnp.zeros_like(carry_ref)

    xn = _rms(x_ref[...], g_ref[...])
    u3_ref[...] = xn.astype(u3_ref.dtype).reshape(u3_ref.shape)
    xh, xm, _ = _split3(xn)
    wh, wm, _ = _split3(wr_ref[...])
    logits = _dot(xh, wh) + _dot(xm, wh) + _dot(xh, wm)
    lane = lax.broadcasted_iota(jnp.int32, logits.shape, 1).astype(F32)
    neg = jnp.float32(-jnp.inf)
    l1 = jnp.where(lane < n_experts, logits, neg)
    m1 = jnp.max(l1, axis=-1, keepdims=True)
    i1 = jnp.min(jnp.where(l1 == m1, lane, float(LANES)), axis=-1, keepdims=True)
    l2 = jnp.where(lane == i1, neg, l1)
    m2 = jnp.max(l2, axis=-1, keepdims=True)
    i2 = jnp.min(jnp.where(l2 == m2, lane, float(LANES)), axis=-1, keepdims=True)
    e2 = jnp.exp(m2 - m1)
    w1 = 1.0 / (1.0 + e2)
    w2 = e2 / (1.0 + e2)
    onehot = jnp.where(lane == i1, 1.0, 0.0) + jnp.where(lane == i2, 1.0, 0.0)
    row = lax.broadcasted_iota(jnp.int32, (tm, tm), 0)
    col = lax.broadcasted_iota(jnp.int32, (tm, tm), 1)
    before = jnp.where(row > col, 1.0, 0.0).astype(BF16)
    excl = _dot(before, onehot.astype(BF16)) + carry_ref[...]
    rank1 = _lane_col(excl, lane, i1)
    rank2 = _lane_col(excl, lane, i2)
    total = carry_ref[...] + jnp.sum(onehot, axis=0, keepdims=True)
    carry_ref[...] = total
    counts_ref[...] = total
    meta = jnp.zeros_like(logits)
    for idx, val in ((META_E1, i1), (META_E2, i2), (META_RANK1, rank1), (META_RANK2, rank2),
                     (META_W1, w1), (META_W2, w2)):
        meta = jnp.where(lane == idx, val, meta)
    meta_ref[...] = meta


def rmsnorm_router(x, g, w_router, tm=256):
    t, d = x.shape
    tm = min(tm, t)
    n_experts = w_router.shape[1]
    wr = jnp.pad(w_router, ((0, 0), (0, LANES - n_experts)))
    return pl.pallas_call(
        functools.partial(_rmsnorm_router_kernel, n_experts=n_experts), grid=(t // tm,),
        in_specs=[pl.BlockSpec((tm, d), lambda i: (i, 0)), pl.BlockSpec((1, d), lambda i: (0, 0)),
                  pl.BlockSpec((d, LANES), lambda i: (0, 0))],
        out_specs=[pl.BlockSpec((tm, d // LANES, LANES), lambda i: (i, 0, 0)),
                   pl.BlockSpec((tm, LANES), lambda i: (i, 0)),
                   pl.BlockSpec((1, LANES), lambda i: (0, 0))],
        out_shape=[jax.ShapeDtypeStruct((t, d // LANES, LANES), BF16), jax.ShapeDtypeStruct((t, LANES), F32),
                   jax.ShapeDtypeStruct((1, LANES), F32)],
        scratch_shapes=[pltpu.VMEM((1, LANES), F32)],
        compiler_params=_params("arbitrary"), name="rmsnorm_router",
    )(x, g.reshape(1, d), wr)


def _mm_kernel(a_ref, b_ref, o_ref):
    o_ref[...] = _dot(a_ref[...], b_ref[...]).astype(o_ref.dtype)


def _cast_kernel(x_ref, o_ref):
    o_ref[...] = x_ref[...].astype(o_ref.dtype)


CAST_BLOCK_BYTES = 8 * 1024 * 1024


def to_bf16(w):
    shape = w.shape
    w2 = w.reshape(-1, shape[-1])
    r, c = w2.shape
    tr = r
    for cand in (1024, 512, 256, 128, 64, 32, 16):
        if r % cand == 0 and cand * c * 4 <= CAST_BLOCK_BYTES:
            tr = cand
            break
    out = pl.pallas_call(
        _cast_kernel, grid=(r // tr,),
        in_specs=[pl.BlockSpec((tr, c), lambda i: (i, 0))],
        out_specs=pl.BlockSpec((tr, c), lambda i: (i, 0)),
        out_shape=jax.ShapeDtypeStruct((r, c), BF16),
        compiler_params=_params("parallel"), name="to_bf16",
    )(w2)
    return out.reshape(shape)


def matmul(a, b, out_dtype, tm, tn, col0=0, n=None):
    m, k = a.shape
    n = b.shape[1] if n is None else n
    tm, tn = min(tm, m), min(tn, n)
    cb0 = col0 // tn
    return pl.pallas_call(
        _mm_kernel, grid=(m // tm, n // tn),
        in_specs=[pl.BlockSpec((tm, k), lambda i, j: (i, 0)), pl.BlockSpec((k, tn), lambda i, j: (0, cb0 + j))],
        out_specs=pl.BlockSpec((tm, tn), lambda i, j: (i, j)),
        out_shape=jax.ShapeDtypeStruct((m, n), out_dtype),
        compiler_params=_params("parallel", "parallel"), name="matmul",
    )(a, b)


def _mm_res_kernel(a_ref, b_ref, r_ref, o_ref):
    o_ref[...] = r_ref[...] + _dot(a_ref[...], b_ref[...])


def matmul_residual_wide_k(a, b, res, tm, tn):
    m, k = a.shape
    n = b.shape[1]
    tm, tn = min(tm, m), min(tn, n)
    return pl.pallas_call(
        _mm_res_kernel, grid=(n // tn, m // tm),
        in_specs=[pl.BlockSpec((tm, k), lambda j, i: (i, 0)),
                  pl.BlockSpec((k, tn), lambda j, i: (0, j), pipeline_mode=pl.Buffered(1)),
                  pl.BlockSpec((tm, tn), lambda j, i: (i, j))],
        out_specs=pl.BlockSpec((tm, tn), lambda j, i: (i, j)),
        out_shape=jax.ShapeDtypeStruct((m, n), F32),
        compiler_params=_params("parallel", "parallel"), name="matmul_residual_wide_k",
    )(a, b, res)


def matmul_residual(a, b, res, tm, tn):
    m, k = a.shape
    n = b.shape[1]
    tm, tn = min(tm, m), min(tn, n)
    return pl.pallas_call(
        _mm_res_kernel, grid=(m // tm, n // tn),
        in_specs=[pl.BlockSpec((tm, k), lambda i, j: (i, 0)), pl.BlockSpec((k, tn), lambda i, j: (0, j)),
                  pl.BlockSpec((tm, tn), lambda i, j: (i, j))],
        out_specs=pl.BlockSpec((tm, tn), lambda i, j: (i, j)),
        out_shape=jax.ShapeDtypeStruct((m, n), F32),
        compiler_params=_params("parallel", "parallel"), name="matmul_residual",
    )(a, b, res)


def _swiglu_kernel(a_ref, wg_ref, wu_ref, o_ref):
    a = a_ref[...]
    g = _dot(a, wg_ref[...])
    u = _dot(a, wu_ref[...])
    o_ref[...] = (_silu(g) * u).astype(o_ref.dtype)


def swiglu_up(a, w_gate, w_up, tm, tn):
    m, k = a.shape
    n = w_gate.shape[1]
    tm, tn = min(tm, m), min(tn, n)
    return pl.pallas_call(
        _swiglu_kernel, grid=(m // tm, n // tn),
        in_specs=[pl.BlockSpec((tm, k), lambda i, j: (i, 0)), pl.BlockSpec((k, tn), lambda i, j: (0, j)),
                  pl.BlockSpec((k, tn), lambda i, j: (0, j))],
        out_specs=pl.BlockSpec((tm, tn), lambda i, j: (i, j)),
        out_shape=jax.ShapeDtypeStruct((m, n), BF16),
        compiler_params=_params("parallel", "parallel"), name="swiglu_up",
    )(a, w_gate, w_up)


def _moe_dispatch_kernel(s1_ref, s2_ref, u3_ref, xs_in_ref, xs_ref, sem, *, r):
    del xs_in_ref
    base = pl.program_id(0) * r

    def copies(k):
        t = base + k
        return [pltpu.make_async_copy(u3_ref.at[k], xs_ref.at[s1_ref[t]], sem),
                pltpu.make_async_copy(u3_ref.at[k], xs_ref.at[s2_ref[t]], sem)]

    def start(k, c):
        for cp in copies(k):
            cp.start()
        return c

    def wait(k, c):
        for cp in copies(k):
            cp.wait()
        return c

    lax.fori_loop(0, r, start, 0)
    lax.fori_loop(0, r, wait, 0)


def moe_dispatch(u3, slot1, slot2, n_slots, r=256):
    t, s, l = u3.shape
    r = min(r, t)
    return pl.pallas_call(
        functools.partial(_moe_dispatch_kernel, r=r),
        grid_spec=pltpu.PrefetchScalarGridSpec(
            num_scalar_prefetch=2, grid=(t // r,),
            in_specs=[pl.BlockSpec((r, s, l), lambda i, s1, s2: (i, 0, 0)), pl.BlockSpec(memory_space=pl.ANY)],
            out_specs=pl.BlockSpec(memory_space=pl.ANY),
            scratch_shapes=[pltpu.SemaphoreType.DMA(())]),
        out_shape=jax.ShapeDtypeStruct((n_slots, s, l), u3.dtype),
        input_output_aliases={3: 0},
        compiler_params=_params("arbitrary"), name="moe_dispatch",
    )(slot1, slot2, u3, jnp.zeros((n_slots, s, l), u3.dtype))


def _moe_up_kernel(te_ref, nu_ref, a3_ref, wg_ref, wu_ref, o_ref, a2_ref):
    i = pl.program_id(0)
    j = pl.program_id(1)

    @pl.when(j == 0)
    def _():
        a2_ref[...] = a3_ref[...].reshape(a2_ref.shape)

    @pl.when(i < nu_ref[0])
    def _():
        a = a2_ref[...]
        g = _dot(a, wg_ref[...])
        u = _dot(a, wu_ref[...])
        o_ref[...] = (_silu(g) * u).astype(o_ref.dtype)

    @pl.when(i >= nu_ref[0])
    def _():
        o_ref[...] = jnp.zeros_like(o_ref)


def moe_up(xs3, tile_expert, n_used, w_gate, w_up, tmg, tn):
    p, s, l = xs3.shape
    n_e, k, f = w_gate.shape
    tn = min(tn, f)
    wspec = pl.BlockSpec((None, k, tn), lambda i, j, te, nu: (te[i], 0, j))
    return pl.pallas_call(
        _moe_up_kernel,
        grid_spec=pltpu.PrefetchScalarGridSpec(
            num_scalar_prefetch=2, grid=(p // tmg, f // tn),
            in_specs=[pl.BlockSpec((tmg, s, l), lambda i, j, te, nu: (i, 0, 0)), wspec, wspec],
            out_specs=pl.BlockSpec((tmg, tn), lambda i, j, te, nu: (i, j)),
            scratch_shapes=[pltpu.VMEM((tmg, k), xs3.dtype)]),
        out_shape=jax.ShapeDtypeStruct((p, f), BF16),
        compiler_params=_params("arbitrary", "arbitrary"), name="moe_up",
    )(tile_expert, n_used, xs3, w_gate, w_up)


def _moe_down_kernel(te_ref, nu_ref, a_ref, wd_ref, o3_ref):
    i = pl.program_id(0)

    @pl.when(i < nu_ref[0])
    def _():
        o3_ref[...] = _dot(a_ref[...], wd_ref[...]).astype(o3_ref.dtype).reshape(o3_ref.shape)

    @pl.when(i >= nu_ref[0])
    def _():
        o3_ref[...] = jnp.zeros_like(o3_ref)


def moe_down(hid, tile_expert, n_used, w_down, tmg):
    p, f = hid.shape
    n_e, _, d = w_down.shape
    return pl.pallas_call(
        _moe_down_kernel,
        grid_spec=pltpu.PrefetchScalarGridSpec(
            num_scalar_prefetch=2, grid=(p // tmg,),
            in_specs=[pl.BlockSpec((tmg, f), lambda i, te, nu: (i, 0)),
                      pl.BlockSpec((None, f, d), lambda i, te, nu: (te[i], 0, 0))],
            out_specs=pl.BlockSpec((tmg, d // LANES, LANES), lambda i, te, nu: (i, 0, 0))),
        out_shape=jax.ShapeDtypeStruct((p, d // LANES, LANES), BF16),
        compiler_params=_params("arbitrary"), name="moe_down",
    )(tile_expert, n_used, hid, w_down)


def _moe_combine_kernel(s1_ref, s2_ref, h_ref, meta_ref, y3_ref, o_ref, ya_ref, yb_ref, sem, *, r):
    base = pl.program_id(0) * r

    def copies(k):
        t = base + k
        return [pltpu.make_async_copy(y3_ref.at[s1_ref[t]], ya_ref.at[k], sem),
                pltpu.make_async_copy(y3_ref.at[s2_ref[t]], yb_ref.at[k], sem)]

    def start(k, c):
        for cp in copies(k):
            cp.start()
        return c

    def wait(k, c):
        for cp in copies(k):
            cp.wait()
        return c

    lax.fori_loop(0, r, start, 0)
    meta = meta_ref[...]
    lane = lax.broadcasted_iota(jnp.int32, meta.shape, 1)
    w1 = _lane_col(meta, lane, META_W1)
    w2 = _lane_col(meta, lane, META_W2)
    lax.fori_loop(0, r, wait, 0)
    d = h_ref.shape[1]
    ya = ya_ref[...].reshape(r, d).astype(F32)
    yb = yb_ref[...].reshape(r, d).astype(F32)
    o_ref[...] = h_ref[...] + w1 * ya + w2 * yb


def moe_combine(h, meta, y3, slot1, slot2, r=256):
    t, d = h.shape
    _, s, l = y3.shape
    r = min(r, t)
    return pl.pallas_call(
        functools.partial(_moe_combine_kernel, r=r),
        grid_spec=pltpu.PrefetchScalarGridSpec(
            num_scalar_prefetch=2, grid=(t // r,),
            in_specs=[pl.BlockSpec((r, d), lambda i, s1, s2: (i, 0)),
                      pl.BlockSpec((r, LANES), lambda i, s1, s2: (i, 0)),
                      pl.BlockSpec(memory_space=pl.ANY)],
            out_specs=pl.BlockSpec((r, d), lambda i, s1, s2: (i, 0)),
            scratch_shapes=[pltpu.VMEM((r, s, l), y3.dtype), pltpu.VMEM((r, s, l), y3.dtype),
                            pltpu.SemaphoreType.DMA(())]),
        out_shape=jax.ShapeDtypeStruct((t, d), F32),
        compiler_params=_params("arbitrary"), name="moe_combine",
    )(slot1, slot2, h, meta, y3)


def _moe_plan(meta, counts, n_experts, tmg, n_tiles):
    i32 = jnp.int32
    e1, e2 = meta[:, META_E1].astype(i32), meta[:, META_E2].astype(i32)
    r1, r2 = meta[:, META_RANK1].astype(i32), meta[:, META_RANK2].astype(i32)
    cnt = counts[0, :n_experts].astype(i32)
    padded = (cnt + tmg - 1) // tmg * tmg
    ends = jnp.cumsum(padded)
    starts = ends - padded
    slot1 = starts[e1] + r1
    slot2 = starts[e2] + r2
    tile_row0 = jnp.arange(n_tiles, dtype=i32) * tmg
    tile_expert = jnp.minimum(jnp.sum(ends[None, :] <= tile_row0[:, None], axis=1), n_experts - 1).astype(i32)
    n_used = (ends[-1:] // tmg).astype(i32)
    return slot1, slot2, tile_expert, n_used


def _softplus(x):
    return jnp.maximum(x, 0.0) + jnp.log1p(jnp.exp(-jnp.abs(x)))


def _conv_silu_kernel(x_ref, halo_ref, w_ref, b_ref, o_ref, buf_ref, *, seq_tiles, k_conv):
    i = pl.program_id(0)
    ts = x_ref.shape[0]
    hl = halo_ref.shape[0]
    x = x_ref[...].astype(F32)
    halo = halo_ref[...].astype(F32)
    buf_ref[0:hl, :] = jnp.where(i % seq_tiles == 0, 0.0, halo)
    buf_ref[hl:hl + ts, :] = x
    acc = x * w_ref[k_conv - 1:k_conv, :] + b_ref[...]
    for s in range(1, k_conv):
        acc = acc + buf_ref[hl - s:hl - s + ts, :] * w_ref[k_conv - 1 - s:k_conv - s, :]
    o_ref[...] = _silu(acc).astype(o_ref.dtype)


def conv_silu(zx, col0, n_cols, conv_w, conv_b, seq, ts=512, tc=1024, hl=16):
    t = zx.shape[0]
    k_conv = conv_w.shape[0]
    ts, tc = min(ts, seq), min(tc, n_cols)
    cb0 = col0 // tc
    return pl.pallas_call(
        functools.partial(_conv_silu_kernel, seq_tiles=seq // ts, k_conv=k_conv),
        grid=(t // ts, n_cols // tc),
        in_specs=[pl.BlockSpec((ts, tc), lambda i, j: (i, cb0 + j)),
                  pl.BlockSpec((hl, tc), lambda i, j: (jnp.maximum(i * (ts // hl) - 1, 0), cb0 + j)),
                  pl.BlockSpec((k_conv, tc), lambda i, j: (0, j)),
                  pl.BlockSpec((1, tc), lambda i, j: (0, j))],
        out_specs=pl.BlockSpec((ts, tc), lambda i, j: (i, j)),
        out_shape=jax.ShapeDtypeStruct((t, n_cols), BF16),
        scratch_shapes=[pltpu.VMEM((hl + ts, tc), F32)],
        compiler_params=_params("parallel", "parallel"), name="conv_silu",
    )(zx, zx, conv_w, conv_b.reshape(1, n_cols))


def _ssd_scan_kernel(x_ref, b_ref, c_ref, z_ref, dt_ref, dtb_ref, alog_ref, dsk_ref, ng_ref, e_ref,
                     o_ref, state_ref, acumT_ref, dtT_ref, y_ref, *, hg):
    g = pl.program_id(1)
    c = pl.program_id(2)
    q = x_ref.shape[0]

    @pl.when(c == 0)
    def _():
        state_ref[...] = jnp.zeros_like(state_ref)

    dt = _softplus(dt_ref[...] + dtb_ref[...])
    a = dt * (-jnp.exp(alog_ref[...]))
    row = lax.broadcasted_iota(jnp.int32, (q, q), 0)
    col = lax.broadcasted_iota(jnp.int32, (q, q), 1)
    causal = row >= col
    tri = jnp.where(causal, 1.0, 0.0).astype(BF16)
    a_hi, a_mid, a_lo = _split3(a)
    acum = _dot(tri, a_hi) + _dot(tri, a_mid) + _dot(tri, a_lo)
    acumT_ref[...] = acum.T
    dtT_ref[...] = dt.T
    a_last = acum[q - 1:q, :]
    w_in = jnp.exp(a_last - acum) * dt
    e_a = jnp.exp(acum)
    expd = _dot(jnp.concatenate([w_in, e_a], axis=0).astype(BF16), e_ref[...])
    w_exp = expd[:q, :]
    e_exp = expd[q:, :]

    bm = b_ref[...]
    cm = c_ref[...]
    cb = _dot_nt(cm, bm)
    lane = lax.broadcasted_iota(jnp.int32, (q, LANES), 1)
    heads_per_tile = LANES // SSD_HEAD_DIM
    for p in range(hg // heads_per_tile):
        ms = []
        for hh in range(heads_per_tile):
            hidx = g * hg + p * heads_per_tile + hh
            colv = jnp.sum(jnp.where(lane == hidx, acum, 0.0), axis=1, keepdims=True)
            rowv = acumT_ref[pl.ds(hidx, 1), :]
            dtr = dtT_ref[pl.ds(hidx, 1), :]
            decay = jnp.where(causal, jnp.exp(jnp.minimum(colv - rowv, 0.0)), 0.0)
            ms.append((cb * decay * dtr).astype(BF16))
        m_cat = jnp.concatenate(ms, axis=1)
        xp = x_ref[:, p * LANES:(p + 1) * LANES].astype(F32)
        x_bd = jnp.concatenate(
            [jnp.where((lane // SSD_HEAD_DIM) == hh, xp, 0.0).astype(BF16) for hh in range(heads_per_tile)],
            axis=0)
        y_ref[:, p * LANES:(p + 1) * LANES] = _dot(m_cat, x_bd)

    st = state_ref[...]
    xf = x_ref[...].astype(F32)
    y_off = _dot(cm, st.astype(BF16)) * e_exp
    xw = (xf * w_exp).astype(BF16)
    bt = bm.astype(F32).T.astype(BF16)
    state_ref[...] = st * e_exp[q - 1:q, :] + _dot(bt, xw)
    y = y_ref[...] + y_off + xf * dsk_ref[...]
    y = y * _silu(z_ref[...].astype(F32))
    y = y * lax.rsqrt(jnp.mean(y * y, axis=-1, keepdims=True) + EPS) * ng_ref[...]
    o_ref[...] = y.astype(o_ref.dtype)


def ssd_scan(xbc, zx, dt_raw, dt_bias, a_log, d_skip, norm_g, batch, seq):
    t = xbc.shape[0]
    heads = dt_raw.shape[1]
    assert heads == LANES
    hg = heads // SSD_GROUPS
    gw = hg * SSD_HEAD_DIM
    d_inner = heads * SSD_HEAD_DIM
    q = SSD_CHUNK
    nc = seq // q
    n = SSD_STATE
    assert gw % LANES == 0 and n == LANES
    b_blk0 = d_inner // n
    c_blk0 = (d_inner + SSD_GROUPS * n) // n
    ch = jnp.arange(gw) // SSD_HEAD_DIM
    expand = (jnp.arange(heads)[None, :, None] ==
              (jnp.arange(SSD_GROUPS)[:, None, None] * hg + ch[None, None, :])).astype(BF16)
    d_exp = jnp.repeat(d_skip, SSD_HEAD_DIM).reshape(1, d_inner)
    rowmap = lambda b, g, c: (b * nc + c)
    return pl.pallas_call(
        functools.partial(_ssd_scan_kernel, hg=hg),
        grid=(batch, SSD_GROUPS, nc),
        in_specs=[pl.BlockSpec((q, gw), lambda b, g, c: (rowmap(b, g, c), g)),
                  pl.BlockSpec((q, n), lambda b, g, c: (rowmap(b, g, c), b_blk0 + g)),
                  pl.BlockSpec((q, n), lambda b, g, c: (rowmap(b, g, c), c_blk0 + g)),
                  pl.BlockSpec((q, gw), lambda b, g, c: (rowmap(b, g, c), g)),
                  pl.BlockSpec((q, heads), lambda b, g, c: (rowmap(b, g, c), 0)),
                  pl.BlockSpec((1, heads), lambda b, g, c: (0, 0)),
                  pl.BlockSpec((1, heads), lambda b, g, c: (0, 0)),
                  pl.BlockSpec((1, gw), lambda b, g, c: (0, g)),
                  pl.BlockSpec((1, gw), lambda b, g, c: (0, g)),
                  pl.BlockSpec((None, heads, gw), lambda b, g, c: (g, 0, 0))],
        out_specs=pl.BlockSpec((q, gw), lambda b, g, c: (rowmap(b, g, c), g)),
        out_shape=jax.ShapeDtypeStruct((t, d_inner), BF16),
        scratch_shapes=[pltpu.VMEM((n, gw), F32), pltpu.VMEM((heads, q), F32), pltpu.VMEM((heads, q), F32),
                        pltpu.VMEM((q, gw), F32)],
        compiler_params=_params("parallel", "parallel", "arbitrary"), name="ssd_scan",
    )(xbc, xbc, xbc, zx, dt_raw, dt_bias.reshape(1, heads), a_log.reshape(1, heads), d_exp,
      norm_g.reshape(1, d_inner), expand)


def _shortconv_in_kernel(a_ref, wb_ref, wc_ref, wh_ref, cw_ref, o_ref, carry_ref, buf_ref, *, seq_tiles, k_conv):
    i = pl.program_id(0)
    j = pl.program_id(1)
    tm = a_ref.shape[0]
    hl = carry_ref.shape[1]
    a = a_ref[...]
    gate_b = _dot(a, wb_ref[...])
    v = _dot(a, wc_ref[...]) * _dot(a, wh_ref[...])
    @pl.when(i % seq_tiles == 0)
    def _():
        buf_ref[0:hl, :] = jnp.zeros((hl, v.shape[1]), F32)

    @pl.when(i % seq_tiles != 0)
    def _():
        buf_ref[0:hl, :] = carry_ref[j]

    buf_ref[hl:hl + tm, :] = v
    carry_ref[j] = v[tm - hl:, :]
    acc = v * cw_ref[k_conv - 1:k_conv, :]
    for s in range(1, k_conv):
        acc = acc + buf_ref[hl - s:hl - s + tm, :] * cw_ref[k_conv - 1 - s:k_conv - s, :]
    o_ref[...] = (gate_b * acc).astype(o_ref.dtype)


def shortconv_in(a, w_in, conv_w, seq, tm=1024, tn=256, hl=8):
    m, k = a.shape
    n = w_in.shape[1] // 3
    k_conv = conv_w.shape[0]
    tm, tn = min(tm, seq), min(tn, n)
    nj = n // tn
    wspecs = [pl.BlockSpec((k, tn), lambda i, j, part=part: (0, part * nj + j)) for part in range(3)]
    return pl.pallas_call(
        functools.partial(_shortconv_in_kernel, seq_tiles=seq // tm, k_conv=k_conv),
        grid=(m // tm, n // tn),
        in_specs=[pl.BlockSpec((tm, k), lambda i, j: (i, 0)), *wspecs,
                  pl.BlockSpec((k_conv, tn), lambda i, j: (0, j))],
        out_specs=pl.BlockSpec((tm, tn), lambda i, j: (i, j)),
        out_shape=jax.ShapeDtypeStruct((m, n), BF16),
        scratch_shapes=[pltpu.VMEM((n // tn, hl, tn), F32), pltpu.VMEM((hl + tm, tn), F32)],
        compiler_params=_params("arbitrary", "arbitrary"), name="shortconv_in",
    )(a, w_in, w_in, w_in, conv_w)


def _pool_kernel(h_ref, halo_ref, g_ref, w_ref, sc_ref, o_ref, buf_ref, *, seq, windows):
    i = pl.program_id(0)
    tm, d = h_ref.shape
    hl = halo_ref.shape[0]
    gw = d // len(windows)
    h = h_ref[...]
    g = g_ref[...]
    xn = _rms(h, g)
    row0 = i * tm
    seq_start = (row0 % seq) == 0
    buf_ref[0:hl, :] = jnp.where(seq_start, 0.0, _rms(halo_ref[...], g))
    buf_ref[hl:hl + tm, :] = xn
    pos = (row0 + lax.broadcasted_iota(jnp.int32, (tm, 1), 0)) % seq
    for gi, w in enumerate(windows):
        cs = slice(gi * gw, (gi + 1) * gw)
        tok = xn[:, cs]
        acc = tok
        for s in range(1, w):
            acc = acc + buf_ref[hl - s:hl - s + tm, cs]
        cnt = jnp.minimum(pos + 1, w).astype(F32)
        pooled = (acc / cnt - tok).astype(BF16)
        y = _dot(pooled, w_ref[gi])
        o_ref[:, cs] = h[:, cs] + y * sc_ref[:, cs]


def pool_mixer_residual(h, norm_g, w_group, scale, seq, tm=256):
    t, d = h.shape
    tm = min(tm, seq)
    hl = POOL_HALO
    n_g, gw, _ = w_group.shape
    return pl.pallas_call(
        functools.partial(_pool_kernel, seq=seq, windows=POOL_WINDOWS),
        grid=(t // tm,),
        in_specs=[pl.BlockSpec((tm, d), lambda i: (i, 0)),
                  pl.BlockSpec((hl, d), lambda i: (jnp.maximum(i * (tm // hl) - 1, 0), 0)),
                  pl.BlockSpec((1, d), lambda i: (0, 0)),
                  pl.BlockSpec((n_g, gw, gw), lambda i: (0, 0, 0)),
                  pl.BlockSpec((1, d), lambda i: (0, 0))],
        out_specs=pl.BlockSpec((tm, d), lambda i: (i, 0)),
        out_shape=jax.ShapeDtypeStruct((t, d), F32),
        scratch_shapes=[pltpu.VMEM((hl + tm, d), F32)],
        compiler_params=_params("parallel"), name="pool_mixer",
    )(h, h, norm_g.reshape(1, d), w_group, scale.reshape(1, d))


def _retention_kernel(q_ref, k_ref, v_ref, g_ref, cos_ref, sin_ref, inner_ref, qd_ref, kd_ref, cd_ref,
                      o_ref, state_ref):
    c = pl.program_id(2)

    @pl.when(c == 0)
    def _():
        state_ref[...] = jnp.zeros_like(state_ref)

    dk = q_ref.shape[1]
    half = dk // 2
    q = inner_ref.shape[0]
    st = state_ref[...]
    for cc in range(q_ref.shape[0] // q):
        rows = slice(cc * q, (cc + 1) * q)
        cos = cos_ref[rows, :]
        sin = sin_ref[rows, :]

        def rot(t):
            t1, t2 = t[:, :half], t[:, half:]
            return jnp.concatenate([t1 * cos - t2 * sin, t1 * sin + t2 * cos], axis=1)

        qr = rot(q_ref[rows, :].astype(F32))
        kr = rot(k_ref[rows, :].astype(F32)) * (dk ** -0.5)
        v = v_ref[rows, :]
        scores = _dot_nt(qr.astype(BF16), kr.astype(BF16)) * inner_ref[...]
        y = _dot(scores.astype(BF16), v) + _dot((qr * qd_ref[...]).astype(BF16), st.astype(BF16))
        kdt = (kr * kd_ref[...]).T.astype(BF16)
        st = st * cd_ref[...] + _dot(kdt, v)
        y = y * lax.rsqrt(jnp.mean(y * y, axis=-1, keepdims=True) + EPS)
        o_ref[rows, :] = (y * _silu(g_ref[rows, :].astype(F32))).astype(o_ref.dtype)
    state_ref[...] = st


def retention(qkvg, batch, seq, chunks_per_step=4):
    t, n_in = qkvg.shape
    hh = RET_HEADS
    dk = n_in // (6 * hh)
    dv = 2 * dk
    q = RET_CHUNK
    rb = q * min(chunks_per_step, seq // q)
    nc = seq // rb
    half = dk // 2
    pos = jnp.arange(seq, dtype=F32)
    inv = ROPE_BASE ** (-jnp.arange(half, dtype=F32) / half)
    ang = pos[:, None] * inv[None]
    cos, sin = jnp.cos(ang), jnp.sin(ang)
    log_gamma = jnp.log1p(-jnp.exp2(-5.0 - jnp.arange(hh, dtype=F32)))
    idx = jnp.arange(q, dtype=F32)
    rel = idx[:, None] - idx[None, :]
    inner = jnp.exp(jnp.where(rel[None] >= 0, rel[None] * log_gamma[:, None, None], -jnp.inf))
    q_decay = jnp.exp((idx + 1.0)[None, :] * log_gamma[:, None])[..., None]
    k_decay = jnp.exp((q - 1.0 - idx)[None, :] * log_gamma[:, None])[..., None]
    c_decay = jnp.exp(q * log_gamma)[:, None, None]
    rowmap = lambda b, h, c: b * nc + c
    return pl.pallas_call(
        _retention_kernel, grid=(batch, hh, nc),
        in_specs=[pl.BlockSpec((rb, dk), lambda b, h, c: (rowmap(b, h, c), h)),
                  pl.BlockSpec((rb, dk), lambda b, h, c: (rowmap(b, h, c), hh + h)),
                  pl.BlockSpec((rb, dv), lambda b, h, c: (rowmap(b, h, c), hh + h)),
                  pl.BlockSpec((rb, dv), lambda b, h, c: (rowmap(b, h, c), 2 * hh + h)),
                  pl.BlockSpec((rb, half), lambda b, h, c: (c, 0)),
                  pl.BlockSpec((rb, half), lambda b, h, c: (c, 0)),
                  pl.BlockSpec((None, q, q), lambda b, h, c: (h, 0, 0)),
                  pl.BlockSpec((None, q, 1), lambda b, h, c: (h, 0, 0)),
                  pl.BlockSpec((None, q, 1), lambda b, h, c: (h, 0, 0)),
                  pl.BlockSpec((None, 1, 1), lambda b, h, c: (h, 0, 0))],
        out_specs=pl.BlockSpec((rb, dv), lambda b, h, c: (rowmap(b, h, c), h)),
        out_shape=jax.ShapeDtypeStruct((t, hh * dv), BF16),
        scratch_shapes=[pltpu.VMEM((dk, dv), F32)],
        compiler_params=_params("parallel", "parallel", "arbitrary"), name="retention",
    )(qkvg, qkvg, qkvg, qkvg, cos, sin, inner, q_decay, k_decay, c_decay)


def _dense_ffn(h, norm_g, w_gate, w_up, w_down):
    u = rmsnorm(h, norm_g, BF16)
    hid = swiglu_up(u, to_bf16(w_gate), to_bf16(w_up), tm=2048, tn=256)
    return matmul_residual_wide_k(hid, to_bf16(w_down), h, tm=512, tn=1024)


def _moe_ffn(h, norm_g, w_router, w_gate, w_up, w_down, tmg=512):
    t, d = h.shape
    n_e = w_router.shape[1]
    tmg = min(tmg, t)
    n_tiles = (2 * t) // tmg + n_e
    u3, meta, counts = rmsnorm_router(h, norm_g, w_router)
    slot1, slot2, tile_expert, n_used = _moe_plan(meta, counts, n_e, tmg, n_tiles)
    xs3 = moe_dispatch(u3, slot1, slot2, n_tiles * tmg)
    hid = moe_up(xs3, tile_expert, n_used, to_bf16(w_gate), to_bf16(w_up), tmg, tn=768)
    y3 = moe_down(hid, tile_expert, n_used, to_bf16(w_down), tmg)
    return moe_combine(h, meta, y3, slot1, slot2)


def kernel(x, norm_mix, norm_ffn, norm_final, ssd_w_in, ssd_conv_w, ssd_conv_b, ssd_dt_bias, ssd_a_log, ssd_d, ssd_norm, ssd_w_out, sc_w_in, sc_conv_w, sc_w_out, pool_w, pool_scale, ret_w_qkvg, ret_w_out, ffn0_w_gate, ffn0_w_up, ffn0_w_down, moe1_router, moe1_w_gate, moe1_w_up, moe1_w_down, ffn2_w_gate, ffn2_w_up, ffn2_w_down, moe3_router, moe3_w_gate, moe3_w_up, moe3_w_down):
    batch, seq, d = x.shape
    t = batch * seq
    h = x.reshape(t, d)

    heads = ssd_dt_bias.shape[0]
    d_inner = heads * SSD_HEAD_DIM
    n_xbc = ssd_conv_w.shape[1]
    u = rmsnorm(h, norm_mix[0], BF16)
    w_in = to_bf16(ssd_w_in)
    zx = matmul(u, w_in, BF16, tm=1024, tn=1024, n=d_inner + n_xbc)
    dt_raw = matmul(u, w_in, F32, tm=1024, tn=LANES, col0=d_inner + n_xbc, n=heads)
    xbc = conv_silu(zx, d_inner, n_xbc, ssd_conv_w, ssd_conv_b, seq)
    y = ssd_scan(xbc, zx, dt_raw, ssd_dt_bias, ssd_a_log, ssd_d, ssd_norm, batch, seq)
    h = matmul_residual(y, to_bf16(ssd_w_out), h, tm=1024, tn=256)
    h = _dense_ffn(h, norm_ffn[0], ffn0_w_gate, ffn0_w_up, ffn0_w_down)

    u = rmsnorm(h, norm_mix[1], BF16)
    y = shortconv_in(u, to_bf16(sc_w_in), sc_conv_w, seq)
    h = matmul_residual(y, to_bf16(sc_w_out), h, tm=1024, tn=512)
    h = _moe_ffn(h, norm_ffn[1], moe1_router, moe1_w_gate, moe1_w_up, moe1_w_down)

    h = pool_mixer_residual(h, norm_mix[2], to_bf16(pool_w), pool_scale, seq)
    h = _dense_ffn(h, norm_ffn[2], ffn2_w_gate, ffn2_w_up, ffn2_w_down)

    u = rmsnorm(h, norm_mix[3], BF16)
    qkvg = matmul(u, to_bf16(ret_w_qkvg), BF16, tm=1024, tn=1024)
    y = retention(qkvg, batch, seq)
    h = matmul_residual(y, to_bf16(ret_w_out), h, tm=1024, tn=256)
    h = _moe_ffn(h, norm_ffn[3], moe3_router, moe3_w_gate, moe3_w_up, moe3_w_down)

    return rmsnorm(h, norm_final, F32).reshape(batch, seq, d)
```

```python
import functools

import jax
import jax.numpy as jnp
from jax import lax
from jax.experimental import pallas as pl
from jax.experimental.pallas import tpu as pltpu

F32 = jnp.float32
BF16 = jnp.bfloat16
EPS = 1e-6
V7X_VMEM_LIMIT_BYTES = 60 * 1024 * 1024
LANES = 128

SSD_HEAD_DIM = 64
SSD_GROUPS = 8
SSD_STATE = 128
SSD_CHUNK = 128
POOL_WINDOWS = (2, 4, 8, 16)
POOL_HALO = 16
RET_HEADS = 16
RET_CHUNK = 128
ROPE_BASE = 10000.0


def _params(*sem):
    return pltpu.CompilerParams(dimension_semantics=sem, vmem_limit_bytes=V7X_VMEM_LIMIT_BYTES)


def _silu(x):
    return x / (1.0 + jnp.exp(-x))


def _dot(a, b):
    return jnp.dot(a, b, preferred_element_type=F32)


def _dot_nt(a, b):
    return lax.dot_general(a, b, (((1,), (1,)), ((), ())), preferred_element_type=F32)


def _split3(x):
    hi = x.astype(BF16)
    r1 = x - hi.astype(F32)
    mid = r1.astype(BF16)
    lo = (r1 - mid.astype(F32)).astype(BF16)
    return hi, mid, lo


def _rms(x, g):
    return x * lax.rsqrt(jnp.mean(x * x, axis=-1, keepdims=True) + EPS) * g


def _rmsnorm_kernel(x_ref, g_ref, o_ref):
    o_ref[...] = _rms(x_ref[...], g_ref[...]).astype(o_ref.dtype)


def rmsnorm(x, g, out_dtype, tm=512):
    t, d = x.shape
    tm = min(tm, t)
    return pl.pallas_call(
        _rmsnorm_kernel, grid=(t // tm,),
        in_specs=[pl.BlockSpec((tm, d), lambda i: (i, 0)), pl.BlockSpec((1, d), lambda i: (0, 0))],
        out_specs=pl.BlockSpec((tm, d), lambda i: (i, 0)),
        out_shape=jax.ShapeDtypeStruct((t, d), out_dtype),
        compiler_params=_params("parallel"), name="rmsnorm",
    )(x, g.reshape(1, d))


META_E1, META_E2, META_RANK1, META_RANK2, META_W1, META_W2 = range(6)


def _lane_col(x, lane, idx):
    return jnp.sum(jnp.where(lane == idx, x, 0.0), axis=1, keepdims=True)


def _rmsnorm_router_kernel(x_ref, g_ref, wr_ref, u3_ref, meta_ref, counts_ref, carry_ref, *, n_experts):
    i = pl.program_id(0)
    tm = x_ref.shape[0]

    @pl.when(i == 0)
    def _():
        carry_ref[...] = jnp.zeros_like(carry_ref)

    xn = _rms(x_ref[...], g_ref[...])
    u3_ref[...] = xn.astype(u3_ref.dtype).reshape(u3_ref.shape)
    xh, xm, _ = _split3(xn)
    wh, wm, _ = _split3(wr_ref[...])
    logits = _dot(xh, wh) + _dot(xm, wh) + _dot(xh, wm)
    lane = lax.broadcasted_iota(jnp.int32, logits.shape, 1).astype(F32)
    neg = jnp.float32(-jnp.inf)
    l1 = jnp.where(lane < n_experts, logits, neg)
    m1 = jnp.max(l1, axis=-1, keepdims=True)
    i1 = jnp.min(jnp.where(l1 == m1, lane, float(LANES)), axis=-1, keepdims=True)
    l2 = jnp.where(lane == i1, neg, l1)
    m2 = jnp.max(l2, axis=-1, keepdims=True)
    i2 = jnp.min(jnp.where(l2 == m2, lane, float(LANES)), axis=-1, keepdims=True)
    e2 = jnp.exp(m2 - m1)
    w1 = 1.0 / (1.0 + e2)
    w2 = e2 / (1.0 + e2)
    onehot = jnp.where(lane == i1, 1.0, 0.0) + jnp.where(lane == i2, 1.0, 0.0)
    row = lax.broadcasted_iota(jnp.int32, (tm, tm), 0)
    col = lax.broadcasted_iota(jnp.int32, (tm, tm), 1)
    before = jnp.where(row > col, 1.0, 0.0).astype(BF16)
    excl = _dot(before, onehot.astype(BF16)) + carry_ref[...]
    rank1 = _lane_col(excl, lane, i1)
    rank2 = _lane_col(excl, lane, i2)
    total = carry_ref[...] + jnp.sum(onehot, axis=0, keepdims=True)
    carry_ref[...] = total
    counts_ref[...] = total
    meta = jnp.zeros_like(logits)
    for idx, val in ((META_E1, i1), (META_E2, i2), (META_RANK1, rank1), (META_RANK2, rank2),
                     (META_W1, w1), (META_W2, w2)):
        meta = jnp.where(lane == idx, val, meta)
    meta_ref[...] = meta


def rmsnorm_router(x, g, w_router, tm=256):
    t, d = x.shape
    tm = min(tm, t)
    n_experts = w_router.shape[1]
    wr = jnp.pad(w_router, ((0, 0), (0, LANES - n_experts)))
    return pl.pallas_call(
        functools.partial(_rmsnorm_router_kernel, n_experts=n_experts), grid=(t // tm,),
        in_specs=[pl.BlockSpec((tm, d), lambda i: (i, 0)), pl.BlockSpec((1, d), lambda i: (0, 0)),
                  pl.BlockSpec((d, LANES), lambda i: (0, 0))],
        out_specs=[pl.BlockSpec((tm, d // LANES, LANES), lambda i: (i, 0, 0)),
                   pl.BlockSpec((tm, LANES), lambda i: (i, 0)),
                   pl.BlockSpec((1, LANES), lambda i: (0, 0))],
        out_shape=[jax.ShapeDtypeStruct((t, d // LANES, LANES), BF16), jax.ShapeDtypeStruct((t, LANES), F32),
                   jax.ShapeDtypeStruct((1, LANES), F32)],
        scratch_shapes=[pltpu.VMEM((1, LANES), F32)],
        compiler_params=_params("arbitrary"), name="rmsnorm_router",
    )(x, g.reshape(1, d), wr)


def _cast_kernel(x_ref, o_ref):
    o_ref[...] = x_ref[...].astype(o_ref.dtype)


CAST_BLOCK_BYTES = 8 * 1024 * 1024


def to_bf16(w):
    shape = w.shape
    w2 = w.reshape(-1, shape[-1])
    r, c = w2.shape
    tr = r
    for cand in (1024, 512, 256, 128, 64, 32, 16):
        if r % cand == 0 and cand * c * 4 <= CAST_BLOCK_BYTES:
            tr = cand
            break
    out = pl.pallas_call(
        _cast_kernel, grid=(r // tr,),
        in_specs=[pl.BlockSpec((tr, c), lambda i: (i, 0))],
        out_specs=pl.BlockSpec((tr, c), lambda i: (i, 0)),
        out_shape=jax.ShapeDtypeStruct((r, c), BF16),
        compiler_params=_params("parallel"), name="to_bf16",
    )(w2)
    return out.reshape(shape)


def _passenger_specs(weights, grid):
    steps, nj = grid[0] * grid[1], grid[1]
    flat, in_specs, out_specs, out_shapes = [], [], [], []
    for w in weights:
        w2 = w.reshape(-1, w.shape[-1])
        r, c = w2.shape
        tr = next(cand for cand in range(16, r + 1, 16) if r % cand == 0 and r // cand <= steps)
        last = r // tr - 1
        imap = lambda i, j, last=last: (jnp.minimum(i * nj + j, last), 0)
        flat.append(w2)
        in_specs.append(pl.BlockSpec((tr, c), imap))
        out_specs.append(pl.BlockSpec((tr, c), imap))
        out_shapes.append(jax.ShapeDtypeStruct((r, c), BF16))
    return flat, in_specs, out_specs, out_shapes


def _with_passengers(body, n_in, n_out, n_p):
    def wrapped(*refs):
        ins, p_ins = refs[:n_in], refs[n_in:n_in + n_p]
        outs = refs[n_in + n_p:n_in + n_p + n_out]
        p_outs = refs[n_in + n_p + n_out:n_in + 2 * n_p + n_out]
        scratch = refs[n_in + 2 * n_p + n_out:]
        for p_in, p_out in zip(p_ins, p_outs):
            p_out[...] = p_in[...].astype(p_out.dtype)
        body(*ins, *outs, *scratch)
    return wrapped


def _hosted_call(body, grid, in_specs, out_specs, out_shapes, args, *, sem, name, scratch_shapes=(), cast=()):
    flat, p_in, p_out, p_shapes = _passenger_specs(cast, grid)
    if cast:
        sem = ("arbitrary",) * len(sem)
    outs = pl.pallas_call(
        _with_passengers(body, len(in_specs), len(out_specs), len(flat)), grid=grid,
        in_specs=list(in_specs) + p_in, out_specs=list(out_specs) + p_out,
        out_shape=list(out_shapes) + p_shapes, scratch_shapes=list(scratch_shapes),
        compiler_params=_params(*sem), name=name)(*args, *flat)
    n_out = len(out_specs)
    return tuple(outs[:n_out]), [o.reshape(w.shape) for o, w in zip(outs[n_out:], cast)]


def _mm_kernel(a_ref, b_ref, o_ref):
    o_ref[...] = _dot(a_ref[...], b_ref[...]).astype(o_ref.dtype)


def matmul(a, b, out_dtype, tm, tn, col0=0, n=None, cast=()):
    m, k = a.shape
    n = b.shape[1] if n is None else n
    tm, tn = min(tm, m), min(tn, n)
    cb0 = col0 // tn
    (out,), casted = _hosted_call(
        _mm_kernel, (m // tm, n // tn),
        [pl.BlockSpec((tm, k), lambda i, j: (i, 0)), pl.BlockSpec((k, tn), lambda i, j: (0, cb0 + j))],
        [pl.BlockSpec((tm, tn), lambda i, j: (i, j))], [jax.ShapeDtypeStruct((m, n), out_dtype)], (a, b),
        sem=("parallel", "parallel"), name="matmul", cast=cast)
    return (out, casted) if cast else out


def _mm_res_kernel(a_ref, b_ref, r_ref, o_ref):
    o_ref[...] = r_ref[...] + _dot(a_ref[...], b_ref[...])


def matmul_residual_wide_k(a, b, res, tm, tn, cast=()):
    m, k = a.shape
    n = b.shape[1]
    tm, tn = min(tm, m), min(tn, n)
    (out,), casted = _hosted_call(
        _mm_res_kernel, (n // tn, m // tm),
        [pl.BlockSpec((tm, k), lambda j, i: (i, 0)),
         pl.BlockSpec((k, tn), lambda j, i: (0, j), pipeline_mode=pl.Buffered(1)),
         pl.BlockSpec((tm, tn), lambda j, i: (i, j))],
        [pl.BlockSpec((tm, tn), lambda j, i: (i, j))], [jax.ShapeDtypeStruct((m, n), F32)], (a, b, res),
        sem=("parallel", "parallel"), name="matmul_residual_wide_k", cast=cast)
    return (out, casted) if cast else out


def _mm_res_norm_kernel(a_ref, b_ref, r_ref, g_ref, o_ref, hg_ref, rs_ref, ssq_ref, *, d):
    j = pl.program_id(1)
    h = r_ref[...] + _dot(a_ref[...], b_ref[...])
    o_ref[...] = h
    hg_ref[...] = (h * g_ref[...]).astype(hg_ref.dtype)
    part = jnp.sum(h * h, axis=1, keepdims=True)

    @pl.when(j == 0)
    def _():
        ssq_ref[...] = part

    @pl.when(j != 0)
    def _():
        ssq_ref[...] += part

    @pl.when(j == pl.num_programs(1) - 1)
    def _():
        rs_ref[...] = lax.rsqrt(ssq_ref[...] / d + EPS)


def matmul_residual_norm(a, b, res, norm_g, tm, tn):
    m, k = a.shape
    n = b.shape[1]
    tm, tn = min(tm, m), min(tn, n)
    return pl.pallas_call(
        functools.partial(_mm_res_norm_kernel, d=n), grid=(m // tm, n // tn),
        in_specs=[pl.BlockSpec((tm, k), lambda i, j: (i, 0)), pl.BlockSpec((k, tn), lambda i, j: (0, j)),
                  pl.BlockSpec((tm, tn), lambda i, j: (i, j)), pl.BlockSpec((1, tn), lambda i, j: (0, j))],
        out_specs=[pl.BlockSpec((tm, tn), lambda i, j: (i, j)), pl.BlockSpec((tm, tn), lambda i, j: (i, j)),
                   pl.BlockSpec((tm, 1), lambda i, j: (i, 0))],
        out_shape=[jax.ShapeDtypeStruct((m, n), F32), jax.ShapeDtypeStruct((m, n), BF16),
                   jax.ShapeDtypeStruct((m, 1), F32)],
        scratch_shapes=[pltpu.VMEM((tm, 1), F32)],
        compiler_params=_params("parallel", "arbitrary"), name="matmul_residual_norm",
    )(a, b, res, norm_g.reshape(1, n))


def matmul_residual(a, b, res, tm, tn, cast=()):
    m, k = a.shape
    n = b.shape[1]
    tm, tn = min(tm, m), min(tn, n)
    (out,), casted = _hosted_call(
        _mm_res_kernel, (m // tm, n // tn),
        [pl.BlockSpec((tm, k), lambda i, j: (i, 0)), pl.BlockSpec((k, tn), lambda i, j: (0, j)),
         pl.BlockSpec((tm, tn), lambda i, j: (i, j))],
        [pl.BlockSpec((tm, tn), lambda i, j: (i, j))], [jax.ShapeDtypeStruct((m, n), F32)], (a, b, res),
        sem=("parallel", "parallel"), name="matmul_residual", cast=cast)
    return (out, casted) if cast else out


def _swiglu_kernel(a_ref, rs_ref, wg_ref, wu_ref, o_ref):
    a = a_ref[...]
    rs = rs_ref[...]
    g = _dot(a, wg_ref[...]) * rs
    u = _dot(a, wu_ref[...]) * rs
    o_ref[...] = (_silu(g) * u).astype(o_ref.dtype)


def swiglu_up(a, row_scale, w_gate, w_up, tm, tn, cast=()):
    m, k = a.shape
    n = w_gate.shape[1]
    tm, tn = min(tm, m), min(tn, n)
    (out,), casted = _hosted_call(
        _swiglu_kernel, (m // tm, n // tn),
        [pl.BlockSpec((tm, k), lambda i, j: (i, 0)), pl.BlockSpec((tm, 1), lambda i, j: (i, 0)),
         pl.BlockSpec((k, tn), lambda i, j: (0, j)), pl.BlockSpec((k, tn), lambda i, j: (0, j))],
        [pl.BlockSpec((tm, tn), lambda i, j: (i, j))], [jax.ShapeDtypeStruct((m, n), BF16)],
        (a, row_scale, w_gate, w_up), sem=("parallel", "parallel"), name="swiglu_up", cast=cast)
    return (out, casted) if cast else out


def _moe_dispatch_kernel(s1_ref, s2_ref, u3_ref, xs_in_ref, xs_ref, sem, *, r):
    del xs_in_ref
    base = pl.program_id(0) * r

    def copies(k):
        t = base + k
        return [pltpu.make_async_copy(u3_ref.at[k], xs_ref.at[s1_ref[t]], sem),
                pltpu.make_async_copy(u3_ref.at[k], xs_ref.at[s2_ref[t]], sem)]

    def start(k, c):
        for cp in copies(k):
            cp.start()
        return c

    def wait(k, c):
        for cp in copies(k):
            cp.wait()
        return c

    lax.fori_loop(0, r, start, 0)
    lax.fori_loop(0, r, wait, 0)


def moe_dispatch(u3, slot1, slot2, n_slots, r=256):
    t, s, l = u3.shape
    r = min(r, t)
    return pl.pallas_call(
        functools.partial(_moe_dispatch_kernel, r=r),
        grid_spec=pltpu.PrefetchScalarGridSpec(
            num_scalar_prefetch=2, grid=(t // r,),
            in_specs=[pl.BlockSpec((r, s, l), lambda i, s1, s2: (i, 0, 0)), pl.BlockSpec(memory_space=pl.ANY)],
            out_specs=pl.BlockSpec(memory_space=pl.ANY),
            scratch_shapes=[pltpu.SemaphoreType.DMA(())]),
        out_shape=jax.ShapeDtypeStruct((n_slots, s, l), u3.dtype),
        input_output_aliases={3: 0},
        compiler_params=_params("arbitrary"), name="moe_dispatch",
    )(slot1, slot2, u3, jnp.zeros((n_slots, s, l), u3.dtype))


def _moe_up_kernel(te_ref, nu_ref, a3_ref, wg_ref, wu_ref, o_ref, a2_ref):
    i = pl.program_id(0)
    j = pl.program_id(1)

    @pl.when(j == 0)
    def _():
        a2_ref[...] = a3_ref[...].reshape(a2_ref.shape)

    @pl.when(i < nu_ref[0])
    def _():
        a = a2_ref[...]
        g = _dot(a, wg_ref[...])
        u = _dot(a, wu_ref[...])
        o_ref[...] = (_silu(g) * u).astype(o_ref.dtype)

    @pl.when(i >= nu_ref[0])
    def _():
        o_ref[...] = jnp.zeros_like(o_ref)


def moe_up(xs3, tile_expert, n_used, w_gate, w_up, tmg, tn):
    p, s, l = xs3.shape
    n_e, k, f = w_gate.shape
    tn = min(tn, f)
    wspec = pl.BlockSpec((None, k, tn), lambda i, j, te, nu: (te[i], 0, j))
    return pl.pallas_call(
        _moe_up_kernel,
        grid_spec=pltpu.PrefetchScalarGridSpec(
            num_scalar_prefetch=2, grid=(p // tmg, f // tn),
            in_specs=[pl.BlockSpec((tmg, s, l), lambda i, j, te, nu: (i, 0, 0)), wspec, wspec],
            out_specs=pl.BlockSpec((tmg, tn), lambda i, j, te, nu: (i, j)),
            scratch_shapes=[pltpu.VMEM((tmg, k), xs3.dtype)]),
        out_shape=jax.ShapeDtypeStruct((p, f), BF16),
        compiler_params=_params("arbitrary", "arbitrary"), name="moe_up",
    )(tile_expert, n_used, xs3, w_gate, w_up)


def _moe_down_kernel(te_ref, nu_ref, a_ref, wd_ref, o3_ref):
    i = pl.program_id(0)

    @pl.when(i < nu_ref[0])
    def _():
        o3_ref[...] = _dot(a_ref[...], wd_ref[...]).astype(o3_ref.dtype).reshape(o3_ref.shape)

    @pl.when(i >= nu_ref[0])
    def _():
        o3_ref[...] = jnp.zeros_like(o3_ref)


def moe_down(hid, tile_expert, n_used, w_down, tmg):
    p, f = hid.shape
    n_e, _, d = w_down.shape
    return pl.pallas_call(
        _moe_down_kernel,
        grid_spec=pltpu.PrefetchScalarGridSpec(
            num_scalar_prefetch=2, grid=(p // tmg,),
            in_specs=[pl.BlockSpec((tmg, f), lambda i, te, nu: (i, 0)),
                      pl.BlockSpec((None, f, d), lambda i, te, nu: (te[i], 0, 0))],
            out_specs=pl.BlockSpec((tmg, d // LANES, LANES), lambda i, te, nu: (i, 0, 0))),
        out_shape=jax.ShapeDtypeStruct((p, d // LANES, LANES), BF16),
        compiler_params=_params("arbitrary"), name="moe_down",
    )(tile_expert, n_used, hid, w_down)


def _moe_combine_kernel(s1_ref, s2_ref, h_ref, meta_ref, y3_ref, g_ref, o_ref, ya_ref, yb_ref, sem, *, r, norm_out):
    base = pl.program_id(0) * r

    def copies(k):
        t = base + k
        return [pltpu.make_async_copy(y3_ref.at[s1_ref[t]], ya_ref.at[k], sem),
                pltpu.make_async_copy(y3_ref.at[s2_ref[t]], yb_ref.at[k], sem)]

    def start(k, c):
        for cp in copies(k):
            cp.start()
        return c

    def wait(k, c):
        for cp in copies(k):
            cp.wait()
        return c

    lax.fori_loop(0, r, start, 0)
    meta = meta_ref[...]
    lane = lax.broadcasted_iota(jnp.int32, meta.shape, 1)
    w1 = _lane_col(meta, lane, META_W1)
    w2 = _lane_col(meta, lane, META_W2)
    lax.fori_loop(0, r, wait, 0)
    d = h_ref.shape[1]
    ya = ya_ref[...].reshape(r, d).astype(F32)
    yb = yb_ref[...].reshape(r, d).astype(F32)
    h_new = h_ref[...] + w1 * ya + w2 * yb
    o_ref[...] = _rms(h_new, g_ref[...]) if norm_out else h_new


def moe_combine(h, meta, y3, slot1, slot2, norm_g=None, r=256):
    t, d = h.shape
    norm_out = norm_g is not None
    g = norm_g if norm_out else jnp.ones((d,), F32)
    _, s, l = y3.shape
    r = min(r, t)
    return pl.pallas_call(
        functools.partial(_moe_combine_kernel, r=r, norm_out=norm_out),
        grid_spec=pltpu.PrefetchScalarGridSpec(
            num_scalar_prefetch=2, grid=(t // r,),
            in_specs=[pl.BlockSpec((r, d), lambda i, s1, s2: (i, 0)),
                      pl.BlockSpec((r, LANES), lambda i, s1, s2: (i, 0)),
                      pl.BlockSpec(memory_space=pl.ANY),
                      pl.BlockSpec((1, d), lambda i, s1, s2: (0, 0))],
            out_specs=pl.BlockSpec((r, d), lambda i, s1, s2: (i, 0)),
            scratch_shapes=[pltpu.VMEM((r, s, l), y3.dtype), pltpu.VMEM((r, s, l), y3.dtype),
                            pltpu.SemaphoreType.DMA(())]),
        out_shape=jax.ShapeDtypeStruct((t, d), F32),
        compiler_params=_params("arbitrary"), name="moe_combine",
    )(slot1, slot2, h, meta, y3, g.reshape(1, d))


def _moe_plan(meta, counts, n_experts, tmg, n_tiles):
    i32 = jnp.int32
    e1, e2 = meta[:, META_E1].astype(i32), meta[:, META_E2].astype(i32)
    r1, r2 = meta[:, META_RANK1].astype(i32), meta[:, META_RANK2].astype(i32)
    cnt = counts[0, :n_experts].astype(i32)
    padded = (cnt + tmg - 1) // tmg * tmg
    ends = jnp.cumsum(padded)
    starts = ends - padded
    slot1 = starts[e1] + r1
    slot2 = starts[e2] + r2
    tile_row0 = jnp.arange(n_tiles, dtype=i32) * tmg
    tile_expert = jnp.minimum(jnp.sum(ends[None, :] <= tile_row0[:, None], axis=1), n_experts - 1).astype(i32)
    n_used = (ends[-1:] // tmg).astype(i32)
    return slot1, slot2, tile_expert, n_used


def _softplus(x):
    return jnp.maximum(x, 0.0) + jnp.log1p(jnp.exp(-jnp.abs(x)))


def _ssd_in_kernel(a_ref, w_ref, cw_ref, cb_ref, o_ref, carry_ref, buf_ref, *, seq_tiles, k_conv, n_plain):
    i = pl.program_id(0)
    j = pl.program_id(1)
    tm = a_ref.shape[0]
    hl = carry_ref.shape[1]
    acc = _dot(a_ref[...], w_ref[...])

    @pl.when(j < n_plain)
    def _():
        o_ref[...] = acc.astype(o_ref.dtype)

    @pl.when(j >= n_plain)
    def _():
        jc = j - n_plain

        @pl.when(i % seq_tiles == 0)
        def _():
            buf_ref[0:hl, :] = jnp.zeros((hl, acc.shape[1]), F32)

        @pl.when(i % seq_tiles != 0)
        def _():
            buf_ref[0:hl, :] = carry_ref[jc]

        buf_ref[hl:hl + tm, :] = acc
        carry_ref[jc] = acc[tm - hl:, :]
        out = acc * cw_ref[k_conv - 1:k_conv, :] + cb_ref[...]
        for sft in range(1, k_conv):
            out = out + buf_ref[hl - sft:hl - sft + tm, :] * cw_ref[k_conv - 1 - sft:k_conv - sft, :]
        o_ref[...] = _silu(out).astype(o_ref.dtype)


def ssd_in_proj(a, w, conv_w, conv_b, n_plain_cols, seq, tm=1024, tn=1024, hl=8, cast=()):
    m, k = a.shape
    k_conv, n_conv_cols = conv_w.shape
    tm, tn = min(tm, seq), min(tn, n_conv_cols)
    n = n_plain_cols + n_conv_cols
    n_plain = n_plain_cols // tn
    cmap = lambda i, j: (0, jnp.maximum(j - n_plain, 0))
    (out,), casted = _hosted_call(
        functools.partial(_ssd_in_kernel, seq_tiles=seq // tm, k_conv=k_conv, n_plain=n_plain),
        (m // tm, n // tn),
        [pl.BlockSpec((tm, k), lambda i, j: (i, 0)), pl.BlockSpec((k, tn), lambda i, j: (0, j)),
         pl.BlockSpec((k_conv, tn), cmap), pl.BlockSpec((1, tn), cmap)],
        [pl.BlockSpec((tm, tn), lambda i, j: (i, j))], [jax.ShapeDtypeStruct((m, n), BF16)],
        (a, w, conv_w, conv_b.reshape(1, n_conv_cols)),
        scratch_shapes=[pltpu.VMEM((n_conv_cols // tn, hl, tn), F32), pltpu.VMEM((hl + tm, tn), F32)],
        sem=("arbitrary", "arbitrary"), name="ssd_in_proj", cast=cast)
    return (out, casted) if cast else out


def _ssd_scan_kernel(x_ref, b_ref, c_ref, z_ref, dt_ref, dtb_ref, alog_ref, dsk_ref, ng_ref, e_ref,
                     o_ref, state_ref, acumT_ref, dtT_ref, y_ref, *, hg):
    g = pl.program_id(1)
    c = pl.program_id(2)
    q = x_ref.shape[0]

    @pl.when(c == 0)
    def _():
        state_ref[...] = jnp.zeros_like(state_ref)

    dt = _softplus(dt_ref[...] + dtb_ref[...])
    a = dt * (-jnp.exp(alog_ref[...]))
    row = lax.broadcasted_iota(jnp.int32, (q, q), 0)
    col = lax.broadcasted_iota(jnp.int32, (q, q), 1)
    causal = row >= col
    tri = jnp.where(causal, 1.0, 0.0).astype(BF16)
    a_hi, a_mid, a_lo = _split3(a)
    acum = _dot(tri, a_hi) + _dot(tri, a_mid) + _dot(tri, a_lo)
    acumT_ref[...] = acum.T
    dtT_ref[...] = dt.T
    a_last = acum[q - 1:q, :]
    w_in = jnp.exp(a_last - acum) * dt
    e_a = jnp.exp(acum)
    expd = _dot(jnp.concatenate([w_in, e_a], axis=0).astype(BF16), e_ref[...])
    w_exp = expd[:q, :]
    e_exp = expd[q:, :]

    bm = b_ref[...]
    cm = c_ref[...]
    cb = _dot_nt(cm, bm)
    lane = lax.broadcasted_iota(jnp.int32, (q, LANES), 1)
    heads_per_tile = LANES // SSD_HEAD_DIM
    for p in range(hg // heads_per_tile):
        ms = []
        for hh in range(heads_per_tile):
            hidx = g * hg + p * heads_per_tile + hh
            colv = jnp.sum(jnp.where(lane == hidx, acum, 0.0), axis=1, keepdims=True)
            rowv = acumT_ref[pl.ds(hidx, 1), :]
            dtr = dtT_ref[pl.ds(hidx, 1), :]
            decay = jnp.where(causal, jnp.exp(jnp.minimum(colv - rowv, 0.0)), 0.0)
            ms.append((cb * decay * dtr).astype(BF16))
        m_cat = jnp.concatenate(ms, axis=1)
        xp = x_ref[:, p * LANES:(p + 1) * LANES].astype(F32)
        x_bd = jnp.concatenate(
            [jnp.where((lane // SSD_HEAD_DIM) == hh, xp, 0.0).astype(BF16) for hh in range(heads_per_tile)],
            axis=0)
        y_ref[:, p * LANES:(p + 1) * LANES] = _dot(m_cat, x_bd)

    st = state_ref[...]
    xf = x_ref[...].astype(F32)
    y_off = _dot(cm, st.astype(BF16)) * e_exp
    xw = (xf * w_exp).astype(BF16)
    bt = bm.astype(F32).T.astype(BF16)
    state_ref[...] = st * e_exp[q - 1:q, :] + _dot(bt, xw)
    y = y_ref[...] + y_off + xf * dsk_ref[...]
    y = y * _silu(z_ref[...].astype(F32))
    y = y * lax.rsqrt(jnp.mean(y * y, axis=-1, keepdims=True) + EPS) * ng_ref[...]
    o_ref[...] = y.astype(o_ref.dtype)


def ssd_scan(zx, dt_raw, dt_bias, a_log, d_skip, norm_g, batch, seq):
    t = zx.shape[0]
    heads = dt_raw.shape[1]
    assert heads == LANES
    hg = heads // SSD_GROUPS
    gw = hg * SSD_HEAD_DIM
    d_inner = heads * SSD_HEAD_DIM
    q = SSD_CHUNK
    nc = seq // q
    n = SSD_STATE
    assert gw % LANES == 0 and n == LANES
    x_blk0 = d_inner // gw
    b_blk0 = 2 * d_inner // n
    c_blk0 = (2 * d_inner + SSD_GROUPS * n) // n
    ch = jnp.arange(gw) // SSD_HEAD_DIM
    expand = (jnp.arange(heads)[None, :, None] ==
              (jnp.arange(SSD_GROUPS)[:, None, None] * hg + ch[None, None, :])).astype(BF16)
    d_exp = jnp.repeat(d_skip, SSD_HEAD_DIM).reshape(1, d_inner)
    rowmap = lambda b, g, c: (b * nc + c)
    return pl.pallas_call(
        functools.partial(_ssd_scan_kernel, hg=hg),
        grid=(batch, SSD_GROUPS, nc),
        in_specs=[pl.BlockSpec((q, gw), lambda b, g, c: (rowmap(b, g, c), x_blk0 + g)),
                  pl.BlockSpec((q, n), lambda b, g, c: (rowmap(b, g, c), b_blk0 + g)),
                  pl.BlockSpec((q, n), lambda b, g, c: (rowmap(b, g, c), c_blk0 + g)),
                  pl.BlockSpec((q, gw), lambda b, g, c: (rowmap(b, g, c), g)),
                  pl.BlockSpec((q, heads), lambda b, g, c: (rowmap(b, g, c), 0)),
                  pl.BlockSpec((1, heads), lambda b, g, c: (0, 0)),
                  pl.BlockSpec((1, heads), lambda b, g, c: (0, 0)),
                  pl.BlockSpec((1, gw), lambda b, g, c: (0, g)),
                  pl.BlockSpec((1, gw), lambda b, g, c: (0, g)),
                  pl.BlockSpec((None, heads, gw), lambda b, g, c: (g, 0, 0))],
        out_specs=pl.BlockSpec((q, gw), lambda b, g, c: (rowmap(b, g, c), g)),
        out_shape=jax.ShapeDtypeStruct((t, d_inner), BF16),
        scratch_shapes=[pltpu.VMEM((n, gw), F32), pltpu.VMEM((heads, q), F32), pltpu.VMEM((heads, q), F32),
                        pltpu.VMEM((q, gw), F32)],
        compiler_params=_params("parallel", "parallel", "arbitrary"), name="ssd_scan",
    )(zx, zx, zx, zx, dt_raw, dt_bias.reshape(1, heads), a_log.reshape(1, heads), d_exp,
      norm_g.reshape(1, d_inner), expand)


def _shortconv_in_kernel(a_ref, wb_ref, wc_ref, wh_ref, cw_ref, o_ref, carry_ref, buf_ref, *, seq_tiles, k_conv):
    i = pl.program_id(0)
    j = pl.program_id(1)
    tm = a_ref.shape[0]
    hl = carry_ref.shape[1]
    a = a_ref[...]
    gate_b = _dot(a, wb_ref[...])
    v = _dot(a, wc_ref[...]) * _dot(a, wh_ref[...])

    @pl.when(i % seq_tiles == 0)
    def _():
        buf_ref[0:hl, :] = jnp.zeros((hl, v.shape[1]), F32)

    @pl.when(i % seq_tiles != 0)
    def _():
        buf_ref[0:hl, :] = carry_ref[j]

    buf_ref[hl:hl + tm, :] = v
    carry_ref[j] = v[tm - hl:, :]
    acc = v * cw_ref[k_conv - 1:k_conv, :]
    for s in range(1, k_conv):
        acc = acc + buf_ref[hl - s:hl - s + tm, :] * cw_ref[k_conv - 1 - s:k_conv - s, :]
    o_ref[...] = (gate_b * acc).astype(o_ref.dtype)


def shortconv_in(a, w_in, conv_w, seq, tm=1024, tn=256, hl=8, cast=()):
    m, k = a.shape
    n = w_in.shape[1] // 3
    k_conv = conv_w.shape[0]
    tm, tn = min(tm, seq), min(tn, n)
    nj = n // tn
    wspecs = [pl.BlockSpec((k, tn), lambda i, j, part=part: (0, part * nj + j)) for part in range(3)]
    (out,), casted = _hosted_call(
        functools.partial(_shortconv_in_kernel, seq_tiles=seq // tm, k_conv=k_conv),
        (m // tm, n // tn),
        [pl.BlockSpec((tm, k), lambda i, j: (i, 0)), *wspecs, pl.BlockSpec((k_conv, tn), lambda i, j: (0, j))],
        [pl.BlockSpec((tm, tn), lambda i, j: (i, j))], [jax.ShapeDtypeStruct((m, n), BF16)],
        (a, w_in, w_in, w_in, conv_w),
        scratch_shapes=[pltpu.VMEM((n // tn, hl, tn), F32), pltpu.VMEM((hl + tm, tn), F32)],
        sem=("arbitrary", "arbitrary"), name="shortconv_in", cast=cast)
    return (out, casted) if cast else out


def _pool_kernel(h_ref, halo_ref, g_ref, w_ref, sc_ref, g2_ref, o_ref, u_ref, buf_ref, *, seq, windows):
    i = pl.program_id(0)
    tm, d = h_ref.shape
    hl = halo_ref.shape[0]
    gw = d // len(windows)
    h = h_ref[...]
    g = g_ref[...]
    xn = _rms(h, g)
    row0 = i * tm
    seq_start = (row0 % seq) == 0
    buf_ref[0:hl, :] = jnp.where(seq_start, 0.0, _rms(halo_ref[...], g))
    buf_ref[hl:hl + tm, :] = xn
    pos = (row0 + lax.broadcasted_iota(jnp.int32, (tm, 1), 0)) % seq
    for gi, w in enumerate(windows):
        cs = slice(gi * gw, (gi + 1) * gw)
        tok = xn[:, cs]
        acc = tok
        for s in range(1, w):
            acc = acc + buf_ref[hl - s:hl - s + tm, cs]
        cnt = jnp.minimum(pos + 1, w).astype(F32)
        pooled = (acc / cnt - tok).astype(BF16)
        y = _dot(pooled, w_ref[gi])
        o_ref[:, cs] = h[:, cs] + y * sc_ref[:, cs]
    u_ref[...] = _rms(o_ref[...], g2_ref[...]).astype(u_ref.dtype)


def pool_mixer_residual(h, norm_g, w_group, scale, next_norm_g, seq, tm=256):
    t, d = h.shape
    tm = min(tm, seq)
    hl = POOL_HALO
    n_g, gw, _ = w_group.shape
    return pl.pallas_call(
        functools.partial(_pool_kernel, seq=seq, windows=POOL_WINDOWS),
        grid=(t // tm,),
        in_specs=[pl.BlockSpec((tm, d), lambda i: (i, 0)),
                  pl.BlockSpec((hl, d), lambda i: (jnp.maximum(i * (tm // hl) - 1, 0), 0)),
                  pl.BlockSpec((1, d), lambda i: (0, 0)),
                  pl.BlockSpec((n_g, gw, gw), lambda i: (0, 0, 0)),
                  pl.BlockSpec((1, d), lambda i: (0, 0)),
                  pl.BlockSpec((1, d), lambda i: (0, 0))],
        out_specs=[pl.BlockSpec((tm, d), lambda i: (i, 0)), pl.BlockSpec((tm, d), lambda i: (i, 0))],
        out_shape=[jax.ShapeDtypeStruct((t, d), F32), jax.ShapeDtypeStruct((t, d), BF16)],
        scratch_shapes=[pltpu.VMEM((hl + tm, d), F32)],
        compiler_params=_params("parallel"), name="pool_mixer",
    )(h, h, norm_g.reshape(1, d), w_group, scale.reshape(1, d), next_norm_g.reshape(1, d))


def _retention_kernel(q_ref, k_ref, v_ref, g_ref, cos_ref, sin_ref, inner_ref, qd_ref, kd_ref, cd_ref,
                      o_ref, state_ref):
    c = pl.program_id(2)

    @pl.when(c == 0)
    def _():
        state_ref[...] = jnp.zeros_like(state_ref)

    dk = q_ref.shape[1]
    half = dk // 2
    q = inner_ref.shape[0]
    st = state_ref[...]
    for cc in range(q_ref.shape[0] // q):
        rows = slice(cc * q, (cc + 1) * q)
        cos = cos_ref[rows, :]
        sin = sin_ref[rows, :]

        def rot(t):
            t1, t2 = t[:, :half], t[:, half:]
            return jnp.concatenate([t1 * cos - t2 * sin, t1 * sin + t2 * cos], axis=1)

        qr = rot(q_ref[rows, :].astype(F32))
        kr = rot(k_ref[rows, :].astype(F32)) * (dk ** -0.5)
        v = v_ref[rows, :]
        scores = _dot_nt(qr.astype(BF16), kr.astype(BF16)) * inner_ref[...]
        y = _dot(scores.astype(BF16), v) + _dot((qr * qd_ref[...]).astype(BF16), st.astype(BF16))
        kdt = (kr * kd_ref[...]).T.astype(BF16)
        st = st * cd_ref[...] + _dot(kdt, v)
        y = y * lax.rsqrt(jnp.mean(y * y, axis=-1, keepdims=True) + EPS)
        o_ref[rows, :] = (y * _silu(g_ref[rows, :].astype(F32))).astype(o_ref.dtype)
    state_ref[...] = st


def retention(qkvg, batch, seq, chunks_per_step=4):
    t, n_in = qkvg.shape
    hh = RET_HEADS
    dk = n_in // (6 * hh)
    dv = 2 * dk
    q = RET_CHUNK
    rb = q * min(chunks_per_step, seq // q)
    nc = seq // rb
    half = dk // 2
    pos = jnp.arange(seq, dtype=F32)
    inv = ROPE_BASE ** (-jnp.arange(half, dtype=F32) / half)
    ang = pos[:, None] * inv[None]
    cos, sin = jnp.cos(ang), jnp.sin(ang)
    log_gamma = jnp.log1p(-jnp.exp2(-5.0 - jnp.arange(hh, dtype=F32)))
    idx = jnp.arange(q, dtype=F32)
    rel = idx[:, None] - idx[None, :]
    inner = jnp.exp(jnp.where(rel[None] >= 0, rel[None] * log_gamma[:, None, None], -jnp.inf))
    q_decay = jnp.exp((idx + 1.0)[None, :] * log_gamma[:, None])[..., None]
    k_decay = jnp.exp((q - 1.0 - idx)[None, :] * log_gamma[:, None])[..., None]
    c_decay = jnp.exp(q * log_gamma)[:, None, None]
    rowmap = lambda b, h, c: b * nc + c
    return pl.pallas_call(
        _retention_kernel, grid=(batch, hh, nc),
        in_specs=[pl.BlockSpec((rb, dk), lambda b, h, c: (rowmap(b, h, c), h)),
                  pl.BlockSpec((rb, dk), lambda b, h, c: (rowmap(b, h, c), hh + h)),
                  pl.BlockSpec((rb, dv), lambda b, h, c: (rowmap(b, h, c), hh + h)),
                  pl.BlockSpec((rb, dv), lambda b, h, c: (rowmap(b, h, c), 2 * hh + h)),
                  pl.BlockSpec((rb, half), lambda b, h, c: (c, 0)),
                  pl.BlockSpec((rb, half), lambda b, h, c: (c, 0)),
                  pl.BlockSpec((None, q, q), lambda b, h, c: (h, 0, 0)),
                  pl.BlockSpec((None, q, 1), lambda b, h, c: (h, 0, 0)),
                  pl.BlockSpec((None, q, 1), lambda b, h, c: (h, 0, 0)),
                  pl.BlockSpec((None, 1, 1), lambda b, h, c: (h, 0, 0))],
        out_specs=pl.BlockSpec((rb, dv), lambda b, h, c: (rowmap(b, h, c), h)),
        out_shape=jax.ShapeDtypeStruct((t, hh * dv), BF16),
        scratch_shapes=[pltpu.VMEM((dk, dv), F32)],
        compiler_params=_params("parallel", "parallel", "arbitrary"), name="retention",
    )(qkvg, qkvg, qkvg, qkvg, cos, sin, inner, q_decay, k_decay, c_decay)


def _moe_ffn(h, norm_g, w_router, w_gate, w_up, w_down, out_norm_g=None, tmg=512):
    t, d = h.shape
    n_e = w_router.shape[1]
    tmg = min(tmg, t)
    n_tiles = (2 * t) // tmg + n_e
    u3, meta, counts = rmsnorm_router(h, norm_g, w_router)
    slot1, slot2, tile_expert, n_used = _moe_plan(meta, counts, n_e, tmg, n_tiles)
    xs3 = moe_dispatch(u3, slot1, slot2, n_tiles * tmg)
    hid = moe_up(xs3, tile_expert, n_used, w_gate, w_up, tmg, tn=768)
    y3 = moe_down(hid, tile_expert, n_used, w_down, tmg)
    return moe_combine(h, meta, y3, slot1, slot2, norm_g=out_norm_g)


def kernel(x, norm_mix, norm_ffn, norm_final, ssd_w_in, ssd_conv_w, ssd_conv_b, ssd_dt_bias, ssd_a_log, ssd_d, ssd_norm, ssd_w_out, sc_w_in, sc_conv_w, sc_w_out, pool_w, pool_scale, ret_w_qkvg, ret_w_out, ffn0_w_gate, ffn0_w_up, ffn0_w_down, moe1_router, moe1_w_gate, moe1_w_up, moe1_w_down, ffn2_w_gate, ffn2_w_up, ffn2_w_down, moe3_router, moe3_w_gate, moe3_w_up, moe3_w_down):
    batch, seq, d = x.shape
    t = batch * seq
    h = x.reshape(t, d)
    unit_scale = jnp.ones((t, 1), F32)

    heads = ssd_dt_bias.shape[0]
    d_inner = heads * SSD_HEAD_DIM
    n_xbc = ssd_conv_w.shape[1]
    u = rmsnorm(h, norm_mix[0], BF16)
    w_in = to_bf16(ssd_w_in)
    zx, (w_ssd_out, w_g0, w_u0) = ssd_in_proj(u, w_in, ssd_conv_w, ssd_conv_b, d_inner, seq,
                                              cast=(ssd_w_out, ffn0_w_gate, ffn0_w_up))
    dt_raw = matmul(u, w_in, F32, tm=1024, tn=LANES, col0=d_inner + n_xbc, n=heads)
    y = ssd_scan(zx, dt_raw, ssd_dt_bias, ssd_a_log, ssd_d, ssd_norm, batch, seq)
    h, hg, rs = matmul_residual_norm(y, w_ssd_out, h, norm_ffn[0], tm=1024, tn=256)
    hid, (w_d0, w_sc_in, w_sc_out) = swiglu_up(hg, rs, w_g0, w_u0, tm=2048, tn=256,
                                               cast=(ffn0_w_down, sc_w_in, sc_w_out))
    h, (w_g1, w_u2) = matmul_residual_wide_k(hid, w_d0, h, tm=512, tn=1024, cast=(moe1_w_gate, ffn2_w_up))

    u = rmsnorm(h, norm_mix[1], BF16)
    y, (w_u1, w_d1) = shortconv_in(u, w_sc_in, sc_conv_w, seq, cast=(moe1_w_up, moe1_w_down))
    h, (w_g2,) = matmul_residual(y, w_sc_out, h, tm=1024, tn=512, cast=(ffn2_w_gate,))
    h = _moe_ffn(h, norm_ffn[1], moe1_router, w_g1, w_u1, w_d1)

    h, u = pool_mixer_residual(h, norm_mix[2], to_bf16(pool_w), pool_scale, norm_ffn[2], seq)
    hid, (w_d2, w_qkvg) = swiglu_up(u, unit_scale, w_g2, w_u2, tm=2048, tn=256, cast=(ffn2_w_down, ret_w_qkvg))
    h, (w_ret_out,) = matmul_residual_wide_k(hid, w_d2, h, tm=512, tn=1024, cast=(ret_w_out,))

    u = rmsnorm(h, norm_mix[3], BF16)
    qkvg, (w_g3, w_u3, w_d3) = matmul(u, w_qkvg, BF16, tm=1024, tn=1024, cast=(moe3_w_gate, moe3_w_up, moe3_w_down))
    y = retention(qkvg, batch, seq)
    h = matmul_residual(y, w_ret_out, h, tm=1024, tn=256)
    out = _moe_ffn(h, norm_ffn[3], moe3_router, w_g3, w_u3, w_d3, out_norm_g=norm_final)
    return out.reshape(batch, seq, d)
```

```python
import functools

import jax
import jax.numpy as jnp
from jax import lax
from jax.experimental import pallas as pl
from jax.experimental.pallas import tpu as pltpu

F32 = jnp.float32
BF16 = jnp.bfloat16
EPS = 1e-6
V7X_VMEM_LIMIT_BYTES = 60 * 1024 * 1024
LANES = 128

SSD_HEAD_DIM = 64
SSD_GROUPS = 8
SSD_STATE = 128
SSD_CHUNK = 128
POOL_WINDOWS = (2, 4, 8, 16)
POOL_HALO = 16
RET_HEADS = 16
RET_CHUNK = 128
ROPE_BASE = 10000.0


def _params(*sem):
    return pltpu.CompilerParams(dimension_semantics=sem, vmem_limit_bytes=V7X_VMEM_LIMIT_BYTES)


def _silu(x):
    return x / (1.0 + jnp.exp(-x))


def _dot(a, b):
    return jnp.dot(a, b, preferred_element_type=F32)


def _dot_nt(a, b):
    return lax.dot_general(a, b, (((1,), (1,)), ((), ())), preferred_element_type=F32)


def _split3(x):
    hi = x.astype(BF16)
    r1 = x - hi.astype(F32)
    mid = r1.astype(BF16)
    lo = (r1 - mid.astype(F32)).astype(BF16)
    return hi, mid, lo


def _rms(x, g):
    return x * lax.rsqrt(jnp.mean(x * x, axis=-1, keepdims=True) + EPS) * g


def _rmsnorm_kernel(x_ref, g_ref, o_ref):
    o_ref[...] = _rms(x_ref[...], g_ref[...]).astype(o_ref.dtype)


def rmsnorm(x, g, out_dtype, tm=512):
    t, d = x.shape
    tm = min(tm, t)
    return pl.pallas_call(
        _rmsnorm_kernel, grid=(t // tm,),
        in_specs=[pl.BlockSpec((tm, d), lambda i: (i, 0)), pl.BlockSpec((1, d), lambda i: (0, 0))],
        out_specs=pl.BlockSpec((tm, d), lambda i: (i, 0)),
        out_shape=jax.ShapeDtypeStruct((t, d), out_dtype),
        compiler_params=_params("parallel"), name="rmsnorm",
    )(x, g.reshape(1, d))


META_E1, META_E2, META_RANK1, META_RANK2, META_W1, META_W2 = range(6)


def _lane_col(x, lane, idx):
    return jnp.sum(jnp.where(lane == idx, x, 0.0), axis=1, keepdims=True)


def _rmsnorm_router_kernel(x_ref, g_ref, wr_ref, u3_ref, meta_ref, counts_ref, carry_ref, *, n_experts):
    i = pl.program_id(0)
    tm = x_ref.shape[0]

    @pl.when(i == 0)
    def _():
        carry_ref[...] = jnp.zeros_like(carry_ref)

    xn = _rms(x_ref[...], g_ref[...])
    u3_ref[...] = xn.astype(u3_ref.dtype).reshape(u3_ref.shape)
    xh, xm, _ = _split3(xn)
    wh, wm, _ = _split3(wr_ref[...])
    logits = _dot(xh, wh) + _dot(xm, wh) + _dot(xh, wm)
    lane = lax.broadcasted_iota(jnp.int32, logits.shape, 1).astype(F32)
    neg = jnp.float32(-jnp.inf)
    l1 = jnp.where(lane < n_experts, logits, neg)
    m1 = jnp.max(l1, axis=-1, keepdims=True)
    i1 = jnp.min(jnp.where(l1 == m1, lane, float(LANES)), axis=-1, keepdims=True)
    l2 = jnp.where(lane == i1, neg, l1)
    m2 = jnp.max(l2, axis=-1, keepdims=True)
    i2 = jnp.min(jnp.where(l2 == m2, lane, float(LANES)), axis=-1, keepdims=True)
    e2 = jnp.exp(m2 - m1)
    w1 = 1.0 / (1.0 + e2)
    w2 = e2 / (1.0 + e2)
    onehot = jnp.where(lane == i1, 1.0, 0.0) + jnp.where(lane == i2, 1.0, 0.0)
    row = lax.broadcasted_iota(jnp.int32, (tm, tm), 0)
    col = lax.broadcasted_iota(jnp.int32, (tm, tm), 1)
    before = jnp.where(row > col, 1.0, 0.0).astype(BF16)
    excl = _dot(before, onehot.astype(BF16)) + carry_ref[...]
    rank1 = _lane_col(excl, lane, i1)
    rank2 = _lane_col(excl, lane, i2)
    total = carry_ref[...] + jnp.sum(onehot, axis=0, keepdims=True)
    carry_ref[...] = total
    counts_ref[...] = total
    meta = jnp.zeros_like(logits)
    for idx, val in ((META_E1, i1), (META_E2, i2), (META_RANK1, rank1), (META_RANK2, rank2),
                     (META_W1, w1), (META_W2, w2)):
        meta = jnp.where(lane == idx, val, meta)
    meta_ref[...] = meta


def rmsnorm_router(x, g, w_router, tm=256):
    t, d = x.shape
    tm = min(tm, t)
    n_experts = w_router.shape[1]
    wr = jnp.pad(w_router, ((0, 0), (0, LANES - n_experts)))
    return pl.pallas_call(
        functools.partial(_rmsnorm_router_kernel, n_experts=n_experts), grid=(t // tm,),
        in_specs=[pl.BlockSpec((tm, d), lambda i: (i, 0)), pl.BlockSpec((1, d), lambda i: (0, 0)),
                  pl.BlockSpec((d, LANES), lambda i: (0, 0))],
        out_specs=[pl.BlockSpec((tm, d // LANES, LANES), lambda i: (i, 0, 0)),
                   pl.BlockSpec((tm, LANES), lambda i: (i, 0)),
                   pl.BlockSpec((1, LANES), lambda i: (0, 0))],
        out_shape=[jax.ShapeDtypeStruct((t, d // LANES, LANES), BF16), jax.ShapeDtypeStruct((t, LANES), F32),
                   jax.ShapeDtypeStruct((1, LANES), F32)],
        scratch_shapes=[pltpu.VMEM((1, LANES), F32)],
        compiler_params=_params("arbitrary"), name="rmsnorm_router",
    )(x, g.reshape(1, d), wr)


def _cast_kernel(x_ref, o_ref):
    o_ref[...] = x_ref[...].astype(o_ref.dtype)


CAST_BLOCK_BYTES = 8 * 1024 * 1024


def to_bf16(w):
    shape = w.shape
    w2 = w.reshape(-1, shape[-1])
    r, c = w2.shape
    tr = r
    for cand in (1024, 512, 256, 128, 64, 32, 16):
        if r % cand == 0 and cand * c * 4 <= CAST_BLOCK_BYTES:
            tr = cand
            break
    out = pl.pallas_call(
        _cast_kernel, grid=(r // tr,),
        in_specs=[pl.BlockSpec((tr, c), lambda i: (i, 0))],
        out_specs=pl.BlockSpec((tr, c), lambda i: (i, 0)),
        out_shape=jax.ShapeDtypeStruct((r, c), BF16),
        compiler_params=_params("parallel"), name="to_bf16",
    )(w2)
    return out.reshape(shape)


def _passenger_specs(weights, grid):
    steps, nj = grid[0] * grid[1], grid[1]
    flat, in_specs, out_specs, out_shapes = [], [], [], []
    for w in weights:
        w2 = w.reshape(-1, w.shape[-1])
        r, c = w2.shape
        tr = next(cand for cand in range(16, r + 1, 16) if r % cand == 0 and r // cand <= steps)
        last = r // tr - 1
        imap = lambda i, j, last=last: (jnp.minimum(i * nj + j, last), 0)
        flat.append(w2)
        in_specs.append(pl.BlockSpec((tr, c), imap))
        out_specs.append(pl.BlockSpec((tr, c), imap))
        out_shapes.append(jax.ShapeDtypeStruct((r, c), BF16))
    return flat, in_specs, out_specs, out_shapes


def _with_passengers(body, n_in, n_out, n_p):
    def wrapped(*refs):
        ins, p_ins = refs[:n_in], refs[n_in:n_in + n_p]
        outs = refs[n_in + n_p:n_in + n_p + n_out]
        p_outs = refs[n_in + n_p + n_out:n_in + 2 * n_p + n_out]
        scratch = refs[n_in + 2 * n_p + n_out:]
        for p_in, p_out in zip(p_ins, p_outs):
            p_out[...] = p_in[...].astype(p_out.dtype)
        body(*ins, *outs, *scratch)
    return wrapped


def _hosted_call(body, grid, in_specs, out_specs, out_shapes, args, *, sem, name, scratch_shapes=(), cast=()):
    flat, p_in, p_out, p_shapes = _passenger_specs(cast, grid)
    if cast:
        sem = ("arbitrary",) * len(sem)
    outs = pl.pallas_call(
        _with_passengers(body, len(in_specs), len(out_specs), len(flat)), grid=grid,
        in_specs=list(in_specs) + p_in, out_specs=list(out_specs) + p_out,
        out_shape=list(out_shapes) + p_shapes, scratch_shapes=list(scratch_shapes),
        compiler_params=_params(*sem), name=name)(*args, *flat)
    n_out = len(out_specs)
    return tuple(outs[:n_out]), [o.reshape(w.shape) for o, w in zip(outs[n_out:], cast)]


def _mm_kernel(a_ref, b_ref, o_ref):
    o_ref[...] = _dot(a_ref[...], b_ref[...]).astype(o_ref.dtype)


def matmul(a, b, out_dtype, tm, tn, col0=0, n=None, cast=()):
    m, k = a.shape
    n = b.shape[1] if n is None else n
    tm, tn = min(tm, m), min(tn, n)
    cb0 = col0 // tn
    (out,), casted = _hosted_call(
        _mm_kernel, (m // tm, n // tn),
        [pl.BlockSpec((tm, k), lambda i, j: (i, 0)), pl.BlockSpec((k, tn), lambda i, j: (0, cb0 + j))],
        [pl.BlockSpec((tm, tn), lambda i, j: (i, j))], [jax.ShapeDtypeStruct((m, n), out_dtype)], (a, b),
        sem=("parallel", "parallel"), name="matmul", cast=cast)
    return (out, casted) if cast else out


def _mm_res_kernel(a_ref, b_ref, r_ref, o_ref):
    o_ref[...] = r_ref[...] + _dot(a_ref[...], b_ref[...])


def matmul_residual_wide_k(a, b, res, tm, tn, cast=()):
    m, k = a.shape
    n = b.shape[1]
    tm, tn = min(tm, m), min(tn, n)
    (out,), casted = _hosted_call(
        _mm_res_kernel, (n // tn, m // tm),
        [pl.BlockSpec((tm, k), lambda j, i: (i, 0)),
         pl.BlockSpec((k, tn), lambda j, i: (0, j), pipeline_mode=pl.Buffered(1)),
         pl.BlockSpec((tm, tn), lambda j, i: (i, j))],
        [pl.BlockSpec((tm, tn), lambda j, i: (i, j))], [jax.ShapeDtypeStruct((m, n), F32)], (a, b, res),
        sem=("parallel", "parallel"), name="matmul_residual_wide_k", cast=cast)
    return (out, casted) if cast else out


def _mm_res_norm_kernel(a_ref, b_ref, r_ref, g_ref, o_ref, hg_ref, rs_ref, ssq_ref, *, d):
    j = pl.program_id(1)
    h = r_ref[...] + _dot(a_ref[...], b_ref[...])
    o_ref[...] = h
    hg_ref[...] = (h * g_ref[...]).astype(hg_ref.dtype)
    part = jnp.sum(h * h, axis=1, keepdims=True)

    @pl.when(j == 0)
    def _():
        ssq_ref[...] = part

    @pl.when(j != 0)
    def _():
        ssq_ref[...] += part

    @pl.when(j == pl.num_programs(1) - 1)
    def _():
        rs_ref[...] = lax.rsqrt(ssq_ref[...] / d + EPS)


def matmul_residual_norm(a, b, res, norm_g, tm, tn):
    m, k = a.shape
    n = b.shape[1]
    tm, tn = min(tm, m), min(tn, n)
    return pl.pallas_call(
        functools.partial(_mm_res_norm_kernel, d=n), grid=(m // tm, n // tn),
        in_specs=[pl.BlockSpec((tm, k), lambda i, j: (i, 0)), pl.BlockSpec((k, tn), lambda i, j: (0, j)),
                  pl.BlockSpec((tm, tn), lambda i, j: (i, j)), pl.BlockSpec((1, tn), lambda i, j: (0, j))],
        out_specs=[pl.BlockSpec((tm, tn), lambda i, j: (i, j)), pl.BlockSpec((tm, tn), lambda i, j: (i, j)),
                   pl.BlockSpec((tm, 1), lambda i, j: (i, 0))],
        out_shape=[jax.ShapeDtypeStruct((m, n), F32), jax.ShapeDtypeStruct((m, n), BF16),
                   jax.ShapeDtypeStruct((m, 1), F32)],
        scratch_shapes=[pltpu.VMEM((tm, 1), F32)],
        compiler_params=_params("parallel", "arbitrary"), name="matmul_residual_norm",
    )(a, b, res, norm_g.reshape(1, n))


def matmul_residual(a, b, res, tm, tn, cast=()):
    m, k = a.shape
    n = b.shape[1]
    tm, tn = min(tm, m), min(tn, n)
    (out,), casted = _hosted_call(
        _mm_res_kernel, (m // tm, n // tn),
        [pl.BlockSpec((tm, k), lambda i, j: (i, 0)), pl.BlockSpec((k, tn), lambda i, j: (0, j)),
         pl.BlockSpec((tm, tn), lambda i, j: (i, j))],
        [pl.BlockSpec((tm, tn), lambda i, j: (i, j))], [jax.ShapeDtypeStruct((m, n), F32)], (a, b, res),
        sem=("parallel", "parallel"), name="matmul_residual", cast=cast)
    return (out, casted) if cast else out


def _swiglu_kernel(a_ref, rs_ref, wg_ref, wu_ref, o_ref):
    a = a_ref[...]
    rs = rs_ref[...]
    g = _dot(a, wg_ref[...]) * rs
    u = _dot(a, wu_ref[...]) * rs
    o_ref[...] = (_silu(g) * u).astype(o_ref.dtype)


def swiglu_up(a, row_scale, w_gate, w_up, tm, tn, cast=()):
    m, k = a.shape
    n = w_gate.shape[1]
    tm, tn = min(tm, m), min(tn, n)
    (out,), casted = _hosted_call(
        _swiglu_kernel, (m // tm, n // tn),
        [pl.BlockSpec((tm, k), lambda i, j: (i, 0)), pl.BlockSpec((tm, 1), lambda i, j: (i, 0)),
         pl.BlockSpec((k, tn), lambda i, j: (0, j)), pl.BlockSpec((k, tn), lambda i, j: (0, j))],
        [pl.BlockSpec((tm, tn), lambda i, j: (i, j))], [jax.ShapeDtypeStruct((m, n), BF16)],
        (a, row_scale, w_gate, w_up), sem=("parallel", "parallel"), name="swiglu_up", cast=cast)
    return (out, casted) if cast else out


def _moe_dispatch_kernel(s1_ref, s2_ref, u3_ref, xs_in_ref, xs_ref, sem, *, r):
    del xs_in_ref
    base = pl.program_id(0) * r

    def copies(k):
        t = base + k
        return [pltpu.make_async_copy(u3_ref.at[k], xs_ref.at[s1_ref[t]], sem),
                pltpu.make_async_copy(u3_ref.at[k], xs_ref.at[s2_ref[t]], sem)]

    def start(k, c):
        for cp in copies(k):
            cp.start()
        return c

    def wait(k, c):
        for cp in copies(k):
            cp.wait()
        return c

    lax.fori_loop(0, r, start, 0)
    lax.fori_loop(0, r, wait, 0)


def moe_dispatch(u3, slot1, slot2, n_slots, r=256):
    t, s, l = u3.shape
    r = min(r, t)
    return pl.pallas_call(
        functools.partial(_moe_dispatch_kernel, r=r),
        grid_spec=pltpu.PrefetchScalarGridSpec(
            num_scalar_prefetch=2, grid=(t // r,),
            in_specs=[pl.BlockSpec((r, s, l), lambda i, s1, s2: (i, 0, 0)), pl.BlockSpec(memory_space=pl.ANY)],
            out_specs=pl.BlockSpec(memory_space=pl.ANY),
            scratch_shapes=[pltpu.SemaphoreType.DMA(())]),
        out_shape=jax.ShapeDtypeStruct((n_slots, s, l), u3.dtype),
        input_output_aliases={3: 0},
        compiler_params=_params("arbitrary"), name="moe_dispatch",
    )(slot1, slot2, u3, jnp.zeros((n_slots, s, l), u3.dtype))


def _moe_up_kernel(te_ref, nu_ref, a3_ref, wg_ref, wu_ref, o_ref, a2_ref):
    i = pl.program_id(0)
    j = pl.program_id(1)

    @pl.when(j == 0)
    def _():
        a2_ref[...] = a3_ref[...].reshape(a2_ref.shape)

    @pl.when(i < nu_ref[0])
    def _():
        a = a2_ref[...]
        g = _dot(a, wg_ref[...])
        u = _dot(a, wu_ref[...])
        o_ref[...] = (_silu(g) * u).astype(o_ref.dtype)

    @pl.when(i >= nu_ref[0])
    def _():
        o_ref[...] = jnp.zeros_like(o_ref)


def moe_up(xs3, tile_expert, n_used, w_gate, w_up, tmg, tn):
    p, s, l = xs3.shape
    n_e, k, f = w_gate.shape
    tn = min(tn, f)
    wspec = pl.BlockSpec((None, k, tn), lambda i, j, te, nu: (te[i], 0, j))
    return pl.pallas_call(
        _moe_up_kernel,
        grid_spec=pltpu.PrefetchScalarGridSpec(
            num_scalar_prefetch=2, grid=(p // tmg, f // tn),
            in_specs=[pl.BlockSpec((tmg, s, l), lambda i, j, te, nu: (i, 0, 0)), wspec, wspec],
            out_specs=pl.BlockSpec((tmg, tn), lambda i, j, te, nu: (i, j)),
            scratch_shapes=[pltpu.VMEM((tmg, k), xs3.dtype)]),
        out_shape=jax.ShapeDtypeStruct((p, f), BF16),
        compiler_params=_params("arbitrary", "arbitrary"), name="moe_up",
    )(tile_expert, n_used, xs3, w_gate, w_up)


def _moe_down_kernel(te_ref, nu_ref, a_ref, wd_ref, o3_ref):
    i = pl.program_id(0)

    @pl.when(i < nu_ref[0])
    def _():
        o3_ref[...] = _dot(a_ref[...], wd_ref[...]).astype(o3_ref.dtype).reshape(o3_ref.shape)

    @pl.when(i >= nu_ref[0])
    def _():
        o3_ref[...] = jnp.zeros_like(o3_ref)


def moe_down(hid, tile_expert, n_used, w_down, tmg):
    p, f = hid.shape
    n_e, _, d = w_down.shape
    return pl.pallas_call(
        _moe_down_kernel,
        grid_spec=pltpu.PrefetchScalarGridSpec(
            num_scalar_prefetch=2, grid=(p // tmg,),
            in_specs=[pl.BlockSpec((tmg, f), lambda i, te, nu: (i, 0)),
                      pl.BlockSpec((None, f, d), lambda i, te, nu: (te[i], 0, 0))],
            out_specs=pl.BlockSpec((tmg, d // LANES, LANES), lambda i, te, nu: (i, 0, 0))),
        out_shape=jax.ShapeDtypeStruct((p, d // LANES, LANES), BF16),
        compiler_params=_params("arbitrary"), name="moe_down",
    )(tile_expert, n_used, hid, w_down)


def _moe_combine_kernel(s1_ref, s2_ref, h_ref, meta_ref, y3_ref, g_ref, o_ref, ya_ref, yb_ref, sem, *, r, norm_out):
    base = pl.program_id(0) * r

    def copies(k):
        t = base + k
        return [pltpu.make_async_copy(y3_ref.at[s1_ref[t]], ya_ref.at[k], sem),
                pltpu.make_async_copy(y3_ref.at[s2_ref[t]], yb_ref.at[k], sem)]

    def start(k, c):
        for cp in copies(k):
            cp.start()
        return c

    def wait(k, c):
        for cp in copies(k):
            cp.wait()
        return c

    lax.fori_loop(0, r, start, 0)
    meta = meta_ref[...]
    lane = lax.broadcasted_iota(jnp.int32, meta.shape, 1)
    w1 = _lane_col(meta, lane, META_W1)
    w2 = _lane_col(meta, lane, META_W2)
    lax.fori_loop(0, r, wait, 0)
    d = h_ref.shape[1]
    ya = ya_ref[...].reshape(r, d).astype(F32)
    yb = yb_ref[...].reshape(r, d).astype(F32)
    h_new = h_ref[...] + w1 * ya + w2 * yb
    o_ref[...] = _rms(h_new, g_ref[...]) if norm_out else h_new


def moe_combine(h, meta, y3, slot1, slot2, norm_g=None, r=256):
    t, d = h.shape
    norm_out = norm_g is not None
    g = norm_g if norm_out else jnp.ones((d,), F32)
    _, s, l = y3.shape
    r = min(r, t)
    return pl.pallas_call(
        functools.partial(_moe_combine_kernel, r=r, norm_out=norm_out),
        grid_spec=pltpu.PrefetchScalarGridSpec(
            num_scalar_prefetch=2, grid=(t // r,),
            in_specs=[pl.BlockSpec((r, d), lambda i, s1, s2: (i, 0)),
                      pl.BlockSpec((r, LANES), lambda i, s1, s2: (i, 0)),
                      pl.BlockSpec(memory_space=pl.ANY),
                      pl.BlockSpec((1, d), lambda i, s1, s2: (0, 0))],
            out_specs=pl.BlockSpec((r, d), lambda i, s1, s2: (i, 0)),
            scratch_shapes=[pltpu.VMEM((r, s, l), y3.dtype), pltpu.VMEM((r, s, l), y3.dtype),
                            pltpu.SemaphoreType.DMA(())]),
        out_shape=jax.ShapeDtypeStruct((t, d), F32),
        compiler_params=_params("arbitrary"), name="moe_combine",
    )(slot1, slot2, h, meta, y3, g.reshape(1, d))


def _moe_plan(meta, counts, n_experts, tmg, n_tiles):
    i32 = jnp.int32
    e1, e2 = meta[:, META_E1].astype(i32), meta[:, META_E2].astype(i32)
    r1, r2 = meta[:, META_RANK1].astype(i32), meta[:, META_RANK2].astype(i32)
    cnt = counts[0, :n_experts].astype(i32)
    padded = (cnt + tmg - 1) // tmg * tmg
    ends = jnp.cumsum(padded)
    starts = ends - padded
    slot1 = starts[e1] + r1
    slot2 = starts[e2] + r2
    tile_row0 = jnp.arange(n_tiles, dtype=i32) * tmg
    tile_expert = jnp.minimum(jnp.sum(ends[None, :] <= tile_row0[:, None], axis=1), n_experts - 1).astype(i32)
    n_used = (ends[-1:] // tmg).astype(i32)
    return slot1, slot2, tile_expert, n_used


def _softplus(x):
    return jnp.maximum(x, 0.0) + jnp.log1p(jnp.exp(-jnp.abs(x)))


def _ssd_in_kernel(a_ref, w_ref, cw_ref, cb_ref, o_ref, carry_ref, buf_ref, *, seq_tiles, k_conv, n_plain):
    i = pl.program_id(0)
    j = pl.program_id(1)
    tm = a_ref.shape[0]
    hl = carry_ref.shape[1]
    acc = _dot(a_ref[...], w_ref[...])

    @pl.when(j < n_plain)
    def _():
        o_ref[...] = acc.astype(o_ref.dtype)

    @pl.when(j >= n_plain)
    def _():
        jc = j - n_plain

        @pl.when(i % seq_tiles == 0)
        def _():
            buf_ref[0:hl, :] = jnp.zeros((hl, acc.shape[1]), F32)

        @pl.when(i % seq_tiles != 0)
        def _():
            buf_ref[0:hl, :] = carry_ref[jc]

        buf_ref[hl:hl + tm, :] = acc
        carry_ref[jc] = acc[tm - hl:, :]
        out = acc * cw_ref[k_conv - 1:k_conv, :] + cb_ref[...]
        for sft in range(1, k_conv):
            out = out + buf_ref[hl - sft:hl - sft + tm, :] * cw_ref[k_conv - 1 - sft:k_conv - sft, :]
        o_ref[...] = _silu(out).astype(o_ref.dtype)


def ssd_in_proj(a, w, conv_w, conv_b, n_plain_cols, seq, tm=1024, tn=1024, hl=8, cast=()):
    m, k = a.shape
    k_conv, n_conv_cols = conv_w.shape
    tm, tn = min(tm, seq), min(tn, n_conv_cols)
    n = n_plain_cols + n_conv_cols
    n_plain = n_plain_cols // tn
    cmap = lambda i, j: (0, jnp.maximum(j - n_plain, 0))
    (out,), casted = _hosted_call(
        functools.partial(_ssd_in_kernel, seq_tiles=seq // tm, k_conv=k_conv, n_plain=n_plain),
        (m // tm, n // tn),
        [pl.BlockSpec((tm, k), lambda i, j: (i, 0)), pl.BlockSpec((k, tn), lambda i, j: (0, j)),
         pl.BlockSpec((k_conv, tn), cmap), pl.BlockSpec((1, tn), cmap)],
        [pl.BlockSpec((tm, tn), lambda i, j: (i, j))], [jax.ShapeDtypeStruct((m, n), BF16)],
        (a, w, conv_w, conv_b.reshape(1, n_conv_cols)),
        scratch_shapes=[pltpu.VMEM((n_conv_cols // tn, hl, tn), F32), pltpu.VMEM((hl + tm, tn), F32)],
        sem=("arbitrary", "arbitrary"), name="ssd_in_proj", cast=cast)
    return (out, casted) if cast else out


def _ssd_scan_kernel(x_ref, b_ref, c_ref, z_ref, dt_ref, dtb_ref, alog_ref, dsk_ref, ng_ref, e_ref,
                     o_ref, state_ref, acum_ref, acumT_ref, dtT_ref, we_ref, y_ref, *, hg):
    c = pl.program_id(1)
    g = pl.program_id(2)
    q = x_ref.shape[0]
    row = lax.broadcasted_iota(jnp.int32, (q, q), 0)
    col = lax.broadcasted_iota(jnp.int32, (q, q), 1)
    causal = row >= col

    @pl.when(c == 0)
    def _():
        state_ref[g] = jnp.zeros(state_ref.shape[1:], F32)

    @pl.when(g == 0)
    def _():
        dt = _softplus(dt_ref[...] + dtb_ref[...])
        a = dt * (-jnp.exp(alog_ref[...]))
        tri = jnp.where(causal, 1.0, 0.0).astype(BF16)
        a_hi, a_mid, a_lo = _split3(a)
        acum = _dot(tri, a_hi) + _dot(tri, a_mid) + _dot(tri, a_lo)
        acum_ref[...] = acum
        acumT_ref[...] = acum.T
        dtT_ref[...] = dt.T
        w_in = jnp.exp(acum[q - 1:q, :] - acum) * dt
        we_ref[...] = jnp.concatenate([w_in, jnp.exp(acum)], axis=0).astype(BF16)

    acum = acum_ref[...]
    expd = _dot(we_ref[...], e_ref[...])
    w_exp = expd[:q, :]
    e_exp = expd[q:, :]

    bm = b_ref[...]
    cm = c_ref[...]
    cb = _dot_nt(cm, bm)
    lane = lax.broadcasted_iota(jnp.int32, (q, LANES), 1)
    heads_per_tile = LANES // SSD_HEAD_DIM
    for p in range(hg // heads_per_tile):
        ms = []
        for hh in range(heads_per_tile):
            hidx = g * hg + p * heads_per_tile + hh
            colv = jnp.sum(jnp.where(lane == hidx, acum, 0.0), axis=1, keepdims=True)
            rowv = acumT_ref[pl.ds(hidx, 1), :]
            dtr = dtT_ref[pl.ds(hidx, 1), :]
            decay = jnp.where(causal, jnp.exp(jnp.minimum(colv - rowv, 0.0)), 0.0)
            ms.append((cb * decay * dtr).astype(BF16))
        m_cat = jnp.concatenate(ms, axis=1)
        xp = x_ref[:, p * LANES:(p + 1) * LANES].astype(F32)
        x_bd = jnp.concatenate(
            [jnp.where((lane // SSD_HEAD_DIM) == hh, xp, 0.0).astype(BF16) for hh in range(heads_per_tile)],
            axis=0)
        y_ref[:, p * LANES:(p + 1) * LANES] = _dot(m_cat, x_bd)

    st = state_ref[g]
    xf = x_ref[...].astype(F32)
    y_off = _dot(cm, st.astype(BF16)) * e_exp
    xw = (xf * w_exp).astype(BF16)
    bt = bm.astype(F32).T.astype(BF16)
    state_ref[g] = st * e_exp[q - 1:q, :] + _dot(bt, xw)
    y = y_ref[...] + y_off + xf * dsk_ref[...]
    y = y * _silu(z_ref[...].astype(F32))
    y = y * lax.rsqrt(jnp.mean(y * y, axis=-1, keepdims=True) + EPS) * ng_ref[...]
    o_ref[...] = y.astype(o_ref.dtype)


def ssd_scan(zx, dt_raw, dt_bias, a_log, d_skip, norm_g, batch, seq):
    t = zx.shape[0]
    heads = dt_raw.shape[1]
    assert heads == LANES
    hg = heads // SSD_GROUPS
    gw = hg * SSD_HEAD_DIM
    d_inner = heads * SSD_HEAD_DIM
    q = SSD_CHUNK
    nc = seq // q
    n = SSD_STATE
    assert gw % LANES == 0 and n == LANES
    x_blk0 = d_inner // gw
    b_blk0 = 2 * d_inner // n
    c_blk0 = (2 * d_inner + SSD_GROUPS * n) // n
    ch = jnp.arange(gw) // SSD_HEAD_DIM
    expand = (jnp.arange(heads)[None, :, None] ==
              (jnp.arange(SSD_GROUPS)[:, None, None] * hg + ch[None, None, :])).astype(BF16)
    d_exp = jnp.repeat(d_skip, SSD_HEAD_DIM).reshape(1, d_inner)
    rowmap = lambda b, c, g: (b * nc + c)
    return pl.pallas_call(
        functools.partial(_ssd_scan_kernel, hg=hg),
        grid=(batch, nc, SSD_GROUPS),
        in_specs=[pl.BlockSpec((q, gw), lambda b, c, g: (rowmap(b, c, g), x_blk0 + g)),
                  pl.BlockSpec((q, n), lambda b, c, g: (rowmap(b, c, g), b_blk0 + g)),
                  pl.BlockSpec((q, n), lambda b, c, g: (rowmap(b, c, g), c_blk0 + g)),
                  pl.BlockSpec((q, gw), lambda b, c, g: (rowmap(b, c, g), g)),
                  pl.BlockSpec((q, heads), lambda b, c, g: (rowmap(b, c, g), 0)),
                  pl.BlockSpec((1, heads), lambda b, c, g: (0, 0)),
                  pl.BlockSpec((1, heads), lambda b, c, g: (0, 0)),
                  pl.BlockSpec((1, gw), lambda b, c, g: (0, g)),
                  pl.BlockSpec((1, gw), lambda b, c, g: (0, g)),
                  pl.BlockSpec((None, heads, gw), lambda b, c, g: (g, 0, 0))],
        out_specs=pl.BlockSpec((q, gw), lambda b, c, g: (rowmap(b, c, g), g)),
        out_shape=jax.ShapeDtypeStruct((t, d_inner), BF16),
        scratch_shapes=[pltpu.VMEM((SSD_GROUPS, n, gw), F32), pltpu.VMEM((q, heads), F32),
                        pltpu.VMEM((heads, q), F32), pltpu.VMEM((heads, q), F32),
                        pltpu.VMEM((2 * q, heads), BF16), pltpu.VMEM((q, gw), F32)],
        compiler_params=_params("parallel", "arbitrary", "arbitrary"), name="ssd_scan",
    )(zx, zx, zx, zx, dt_raw, dt_bias.reshape(1, heads), a_log.reshape(1, heads), d_exp,
      norm_g.reshape(1, d_inner), expand)


def _shortconv_in_kernel(a_ref, wb_ref, wc_ref, wh_ref, cw_ref, o_ref, carry_ref, buf_ref, *, seq_tiles, k_conv):
    i = pl.program_id(0)
    j = pl.program_id(1)
    tm = a_ref.shape[0]
    hl = carry_ref.shape[1]
    a = a_ref[...]
    gate_b = _dot(a, wb_ref[...])
    v = _dot(a, wc_ref[...]) * _dot(a, wh_ref[...])

    @pl.when(i % seq_tiles == 0)
    def _():
        buf_ref[0:hl, :] = jnp.zeros((hl, v.shape[1]), F32)

    @pl.when(i % seq_tiles != 0)
    def _():
        buf_ref[0:hl, :] = carry_ref[j]

    buf_ref[hl:hl + tm, :] = v
    carry_ref[j] = v[tm - hl:, :]
    acc = v * cw_ref[k_conv - 1:k_conv, :]
    for s in range(1, k_conv):
        acc = acc + buf_ref[hl - s:hl - s + tm, :] * cw_ref[k_conv - 1 - s:k_conv - s, :]
    o_ref[...] = (gate_b * acc).astype(o_ref.dtype)


def shortconv_in(a, w_in, conv_w, seq, tm=1024, tn=256, hl=8, cast=()):
    m, k = a.shape
    n = w_in.shape[1] // 3
    k_conv = conv_w.shape[0]
    tm, tn = min(tm, seq), min(tn, n)
    nj = n // tn
    wspecs = [pl.BlockSpec((k, tn), lambda i, j, part=part: (0, part * nj + j)) for part in range(3)]
    (out,), casted = _hosted_call(
        functools.partial(_shortconv_in_kernel, seq_tiles=seq // tm, k_conv=k_conv),
        (m // tm, n // tn),
        [pl.BlockSpec((tm, k), lambda i, j: (i, 0)), *wspecs, pl.BlockSpec((k_conv, tn), lambda i, j: (0, j))],
        [pl.BlockSpec((tm, tn), lambda i, j: (i, j))], [jax.ShapeDtypeStruct((m, n), BF16)],
        (a, w_in, w_in, w_in, conv_w),
        scratch_shapes=[pltpu.VMEM((n // tn, hl, tn), F32), pltpu.VMEM((hl + tm, tn), F32)],
        sem=("arbitrary", "arbitrary"), name="shortconv_in", cast=cast)
    return (out, casted) if cast else out


def _pool_kernel(h_ref, halo_ref, g_ref, w_ref, sc_ref, g2_ref, o_ref, u_ref, buf_ref, *, seq, windows):
    i = pl.program_id(0)
    tm, d = h_ref.shape
    hl = halo_ref.shape[0]
    gw = d // len(windows)
    h = h_ref[...]
    g = g_ref[...]
    xn = _rms(h, g)
    row0 = i * tm
    seq_start = (row0 % seq) == 0
    buf_ref[0:hl, :] = jnp.where(seq_start, 0.0, _rms(halo_ref[...], g))
    buf_ref[hl:hl + tm, :] = xn
    pos = (row0 + lax.broadcasted_iota(jnp.int32, (tm, 1), 0)) % seq
    for gi, w in enumerate(windows):
        cs = slice(gi * gw, (gi + 1) * gw)
        tok = xn[:, cs]
        acc = tok
        for s in range(1, w):
            acc = acc + buf_ref[hl - s:hl - s + tm, cs]
        cnt = jnp.minimum(pos + 1, w).astype(F32)
        pooled = (acc / cnt - tok).astype(BF16)
        y = _dot(pooled, w_ref[gi])
        o_ref[:, cs] = h[:, cs] + y * sc_ref[:, cs]
    u_ref[...] = _rms(o_ref[...], g2_ref[...]).astype(u_ref.dtype)


def pool_mixer_residual(h, norm_g, w_group, scale, next_norm_g, seq, tm=256):
    t, d = h.shape
    tm = min(tm, seq)
    hl = POOL_HALO
    n_g, gw, _ = w_group.shape
    return pl.pallas_call(
        functools.partial(_pool_kernel, seq=seq, windows=POOL_WINDOWS),
        grid=(t // tm,),
        in_specs=[pl.BlockSpec((tm, d), lambda i: (i, 0)),
                  pl.BlockSpec((hl, d), lambda i: (jnp.maximum(i * (tm // hl) - 1, 0), 0)),
                  pl.BlockSpec((1, d), lambda i: (0, 0)),
                  pl.BlockSpec((n_g, gw, gw), lambda i: (0, 0, 0)),
                  pl.BlockSpec((1, d), lambda i: (0, 0)),
                  pl.BlockSpec((1, d), lambda i: (0, 0))],
        out_specs=[pl.BlockSpec((tm, d), lambda i: (i, 0)), pl.BlockSpec((tm, d), lambda i: (i, 0))],
        out_shape=[jax.ShapeDtypeStruct((t, d), F32), jax.ShapeDtypeStruct((t, d), BF16)],
        scratch_shapes=[pltpu.VMEM((hl + tm, d), F32)],
        compiler_params=_params("parallel"), name="pool_mixer",
    )(h, h, norm_g.reshape(1, d), w_group, scale.reshape(1, d), next_norm_g.reshape(1, d))


def _retention_kernel(q_ref, k_ref, v_ref, g_ref, cos_ref, sin_ref, inner_ref, qd_ref, kd_ref, cd_ref,
                      o_ref, state_ref):
    c = pl.program_id(2)

    @pl.when(c == 0)
    def _():
        state_ref[...] = jnp.zeros_like(state_ref)

    dk = q_ref.shape[1]
    half = dk // 2
    q = inner_ref.shape[0]
    st = state_ref[...]
    for cc in range(q_ref.shape[0] // q):
        rows = slice(cc * q, (cc + 1) * q)
        cos = cos_ref[rows, :]
        sin = sin_ref[rows, :]

        def rot(t):
            t1, t2 = t[:, :half], t[:, half:]
            return jnp.concatenate([t1 * cos - t2 * sin, t1 * sin + t2 * cos], axis=1)

        qr = rot(q_ref[rows, :].astype(F32))
        kr = rot(k_ref[rows, :].astype(F32)) * (dk ** -0.5)
        v = v_ref[rows, :]
        scores = _dot_nt(qr.astype(BF16), kr.astype(BF16)) * inner_ref[...]
        y = _dot(scores.astype(BF16), v) + _dot((qr * qd_ref[...]).astype(BF16), st.astype(BF16))
        kdt = (kr * kd_ref[...]).T.astype(BF16)
        st = st * cd_ref[...] + _dot(kdt, v)
        y = y * lax.rsqrt(jnp.mean(y * y, axis=-1, keepdims=True) + EPS)
        o_ref[rows, :] = (y * _silu(g_ref[rows, :].astype(F32))).astype(o_ref.dtype)
    state_ref[...] = st


def retention(qkvg, batch, seq, chunks_per_step=8):
    t, n_in = qkvg.shape
    hh = RET_HEADS
    dk = n_in // (6 * hh)
    dv = 2 * dk
    q = RET_CHUNK
    rb = q * min(chunks_per_step, seq // q)
    nc = seq // rb
    half = dk // 2
    pos = jnp.arange(seq, dtype=F32)
    inv = ROPE_BASE ** (-jnp.arange(half, dtype=F32) / half)
    ang = pos[:, None] * inv[None]
    cos, sin = jnp.cos(ang), jnp.sin(ang)
    log_gamma = jnp.log1p(-jnp.exp2(-5.0 - jnp.arange(hh, dtype=F32)))
    idx = jnp.arange(q, dtype=F32)
    rel = idx[:, None] - idx[None, :]
    inner = jnp.exp(jnp.where(rel[None] >= 0, rel[None] * log_gamma[:, None, None], -jnp.inf))
    q_decay = jnp.exp((idx + 1.0)[None, :] * log_gamma[:, None])[..., None]
    k_decay = jnp.exp((q - 1.0 - idx)[None, :] * log_gamma[:, None])[..., None]
    c_decay = jnp.exp(q * log_gamma)[:, None, None]
    rowmap = lambda b, h, c: b * nc + c
    return pl.pallas_call(
        _retention_kernel, grid=(batch, hh, nc),
        in_specs=[pl.BlockSpec((rb, dk), lambda b, h, c: (rowmap(b, h, c), h)),
                  pl.BlockSpec((rb, dk), lambda b, h, c: (rowmap(b, h, c), hh + h)),
                  pl.BlockSpec((rb, dv), lambda b, h, c: (rowmap(b, h, c), hh + h)),
                  pl.BlockSpec((rb, dv), lambda b, h, c: (rowmap(b, h, c), 2 * hh + h)),
                  pl.BlockSpec((rb, half), lambda b, h, c: (c, 0)),
                  pl.BlockSpec((rb, half), lambda b, h, c: (c, 0)),
                  pl.BlockSpec((None, q, q), lambda b, h, c: (h, 0, 0)),
                  pl.BlockSpec((None, q, 1), lambda b, h, c: (h, 0, 0)),
                  pl.BlockSpec((None, q, 1), lambda b, h, c: (h, 0, 0)),
                  pl.BlockSpec((None, 1, 1), lambda b, h, c: (h, 0, 0))],
        out_specs=pl.BlockSpec((rb, dv), lambda b, h, c: (rowmap(b, h, c), h)),
        out_shape=jax.ShapeDtypeStruct((t, hh * dv), BF16),
        scratch_shapes=[pltpu.VMEM((dk, dv), F32)],
        compiler_params=_params("parallel", "parallel", "arbitrary"), name="retention",
    )(qkvg, qkvg, qkvg, qkvg, cos, sin, inner, q_decay, k_decay, c_decay)


def _moe_ffn(h, norm_g, w_router, w_gate, w_up, w_down, out_norm_g=None, tmg=512):
    t, d = h.shape
    n_e = w_router.shape[1]
    tmg = min(tmg, t)
    n_tiles = (2 * t) // tmg + n_e
    u3, meta, counts = rmsnorm_router(h, norm_g, w_router)
    slot1, slot2, tile_expert, n_used = _moe_plan(meta, counts, n_e, tmg, n_tiles)
    xs3 = moe_dispatch(u3, slot1, slot2, n_tiles * tmg)
    hid = moe_up(xs3, tile_expert, n_used, w_gate, w_up, tmg, tn=768)
    y3 = moe_down(hid, tile_expert, n_used, w_down, tmg)
    return moe_combine(h, meta, y3, slot1, slot2, norm_g=out_norm_g)


def kernel(x, norm_mix, norm_ffn, norm_final, ssd_w_in, ssd_conv_w, ssd_conv_b, ssd_dt_bias, ssd_a_log, ssd_d, ssd_norm, ssd_w_out, sc_w_in, sc_conv_w, sc_w_out, pool_w, pool_scale, ret_w_qkvg, ret_w_out, ffn0_w_gate, ffn0_w_up, ffn0_w_down, moe1_router, moe1_w_gate, moe1_w_up, moe1_w_down, ffn2_w_gate, ffn2_w_up, ffn2_w_down, moe3_router, moe3_w_gate, moe3_w_up, moe3_w_down):
    batch, seq, d = x.shape
    t = batch * seq
    h = x.reshape(t, d)
    unit_scale = jnp.ones((t, 1), F32)

    heads = ssd_dt_bias.shape[0]
    d_inner = heads * SSD_HEAD_DIM
    n_xbc = ssd_conv_w.shape[1]
    u = rmsnorm(h, norm_mix[0], BF16)
    w_in = to_bf16(ssd_w_in)
    zx, (w_ssd_out, w_g0, w_u0) = ssd_in_proj(u, w_in, ssd_conv_w, ssd_conv_b, d_inner, seq,
                                              cast=(ssd_w_out, ffn0_w_gate, ffn0_w_up))
    dt_raw = matmul(u, w_in, F32, tm=1024, tn=LANES, col0=d_inner + n_xbc, n=heads)
    y = ssd_scan(zx, dt_raw, ssd_dt_bias, ssd_a_log, ssd_d, ssd_norm, batch, seq)
    h, hg, rs = matmul_residual_norm(y, w_ssd_out, h, norm_ffn[0], tm=1024, tn=256)
    hid, (w_d0, w_sc_in, w_sc_out) = swiglu_up(hg, rs, w_g0, w_u0, tm=2048, tn=256,
                                               cast=(ffn0_w_down, sc_w_in, sc_w_out))
    h, (w_g1, w_u2) = matmul_residual_wide_k(hid, w_d0, h, tm=512, tn=1024, cast=(moe1_w_gate, ffn2_w_up))

    u = rmsnorm(h, norm_mix[1], BF16)
    y, (w_u1, w_d1) = shortconv_in(u, w_sc_in, sc_conv_w, seq, cast=(moe1_w_up, moe1_w_down))
    h, (w_g2,) = matmul_residual_wide_k(y, w_sc_out, h, tm=1024, tn=1024, cast=(ffn2_w_gate,))
    h = _moe_ffn(h, norm_ffn[1], moe1_router, w_g1, w_u1, w_d1)

    h, u = pool_mixer_residual(h, norm_mix[2], to_bf16(pool_w), pool_scale, norm_ffn[2], seq)
    hid, (w_d2, w_qkvg) = swiglu_up(u, unit_scale, w_g2, w_u2, tm=2048, tn=256, cast=(ffn2_w_down, ret_w_qkvg))
    h, (w_ret_out,) = matmul_residual_wide_k(hid, w_d2, h, tm=512, tn=1024, cast=(ret_w_out,))

    u = rmsnorm(h, norm_mix[3], BF16)
    qkvg, (w_g3, w_u3, w_d3) = matmul(u, w_qkvg, BF16, tm=1024, tn=1024, cast=(moe3_w_gate, moe3_w_up, moe3_w_down))
    y = retention(qkvg, batch, seq)
    h = matmul_residual_wide_k(y, w_ret_out, h, tm=512, tn=1024)
    out = _moe_ffn(h, norm_ffn[3], moe3_router, w_g3, w_u3, w_d3, out_norm_g=norm_final)
    return out.reshape(batch, seq, d)
```

```python
import functools

import jax
import jax.numpy as jnp
from jax import lax
from jax.experimental import pallas as pl
from jax.experimental.pallas import tpu as pltpu

F32 = jnp.float32
BF16 = jnp.bfloat16
EPS = 1e-6
V7X_VMEM_LIMIT_BYTES = 60 * 1024 * 1024
LANES = 128

SSD_HEAD_DIM = 64
SSD_GROUPS = 8
SSD_STATE = 128
SSD_CHUNK = 128
POOL_WINDOWS = (2, 4, 8, 16)
POOL_HALO = 16
RET_HEADS = 16
RET_CHUNK = 128
ROPE_BASE = 10000.0


def _params(*sem):
    return pltpu.CompilerParams(dimension_semantics=sem, vmem_limit_bytes=V7X_VMEM_LIMIT_BYTES)


def _silu(x):
    return x / (1.0 + jnp.exp(-x))


def _dot(a, b):
    return jnp.dot(a, b, preferred_element_type=F32)


def _dot_nt(a, b):
    return lax.dot_general(a, b, (((1,), (1,)), ((), ())), preferred_element_type=F32)


def _split3(x):
    hi = x.astype(BF16)
    r1 = x - hi.astype(F32)
    mid = r1.astype(BF16)
    lo = (r1 - mid.astype(F32)).astype(BF16)
    return hi, mid, lo


def _rms(x, g):
    return x * lax.rsqrt(jnp.mean(x * x, axis=-1, keepdims=True) + EPS) * g


def _rmsnorm_kernel(x_ref, g_ref, o_ref):
    o_ref[...] = _rms(x_ref[...], g_ref[...]).astype(o_ref.dtype)


def rmsnorm(x, g, out_dtype, tm=512):
    t, d = x.shape
    tm = min(tm, t)
    return pl.pallas_call(
        _rmsnorm_kernel, grid=(t // tm,),
        in_specs=[pl.BlockSpec((tm, d), lambda i: (i, 0)), pl.BlockSpec((1, d), lambda i: (0, 0))],
        out_specs=pl.BlockSpec((tm, d), lambda i: (i, 0)),
        out_shape=jax.ShapeDtypeStruct((t, d), out_dtype),
        compiler_params=_params("parallel"), name="rmsnorm",
    )(x, g.reshape(1, d))


META_E1, META_E2, META_RANK1, META_RANK2, META_W1, META_W2 = range(6)


def _lane_col(x, lane, idx):
    return jnp.sum(jnp.where(lane == idx, x, 0.0), axis=1, keepdims=True)


def _rmsnorm_router_kernel(x_ref, g_ref, wr_ref, u3_ref, meta_ref, counts_ref, carry_ref, *, n_experts):
    i = pl.program_id(0)
    tm = x_ref.shape[0]

    @pl.when(i == 0)
    def _():
        carry_ref[...] = jnp.zeros_like(carry_ref)

    xn = _rms(x_ref[...], g_ref[...])
    u3_ref[...] = xn.astype(u3_ref.dtype).reshape(u3_ref.shape)
    xh, xm, _ = _split3(xn)
    wh, wm, _ = _split3(wr_ref[...])
    logits = _dot(xh, wh) + _dot(xm, wh) + _dot(xh, wm)
    lane = lax.broadcasted_iota(jnp.int32, logits.shape, 1).astype(F32)
    neg = jnp.float32(-jnp.inf)
    l1 = jnp.where(lane < n_experts, logits, neg)
    m1 = jnp.max(l1, axis=-1, keepdims=True)
    i1 = jnp.min(jnp.where(l1 == m1, lane, float(LANES)), axis=-1, keepdims=True)
    l2 = jnp.where(lane == i1, neg, l1)
    m2 = jnp.max(l2, axis=-1, keepdims=True)
    i2 = jnp.min(jnp.where(l2 == m2, lane, float(LANES)), axis=-1, keepdims=True)
    e2 = jnp.exp(m2 - m1)
    w1 = 1.0 / (1.0 + e2)
    w2 = e2 / (1.0 + e2)
    onehot = jnp.where(lane == i1, 1.0, 0.0) + jnp.where(lane == i2, 1.0, 0.0)
    row = lax.broadcasted_iota(jnp.int32, (tm, tm), 0)
    col = lax.broadcasted_iota(jnp.int32, (tm, tm), 1)
    before = jnp.where(row > col, 1.0, 0.0).astype(BF16)
    excl = _dot(before, onehot.astype(BF16)) + carry_ref[...]
    rank1 = _lane_col(excl, lane, i1)
    rank2 = _lane_col(excl, lane, i2)
    total = carry_ref[...] + jnp.sum(onehot, axis=0, keepdims=True)
    carry_ref[...] = total
    counts_ref[...] = total
    meta = jnp.zeros_like(logits)
    for idx, val in ((META_E1, i1), (META_E2, i2), (META_RANK1, rank1), (META_RANK2, rank2),
                     (META_W1, w1), (META_W2, w2)):
        meta = jnp.where(lane == idx, val, meta)
    meta_ref[...] = meta


def rmsnorm_router(x, g, w_router, tm=256):
    t, d = x.shape
    tm = min(tm, t)
    n_experts = w_router.shape[1]
    wr = jnp.pad(w_router, ((0, 0), (0, LANES - n_experts)))
    return pl.pallas_call(
        functools.partial(_rmsnorm_router_kernel, n_experts=n_experts), grid=(t // tm,),
        in_specs=[pl.BlockSpec((tm, d), lambda i: (i, 0)), pl.BlockSpec((1, d), lambda i: (0, 0)),
                  pl.BlockSpec((d, LANES), lambda i: (0, 0))],
        out_specs=[pl.BlockSpec((tm, d // LANES, LANES), lambda i: (i, 0, 0)),
                   pl.BlockSpec((tm, LANES), lambda i: (i, 0)),
                   pl.BlockSpec((1, LANES), lambda i: (0, 0))],
        out_shape=[jax.ShapeDtypeStruct((t, d // LANES, LANES), BF16), jax.ShapeDtypeStruct((t, LANES), F32),
                   jax.ShapeDtypeStruct((1, LANES), F32)],
        scratch_shapes=[pltpu.VMEM((1, LANES), F32)],
        compiler_params=_params("arbitrary"), name="rmsnorm_router",
    )(x, g.reshape(1, d), wr)


def _cast_kernel(x_ref, o_ref):
    o_ref[...] = x_ref[...].astype(o_ref.dtype)


CAST_BLOCK_BYTES = 8 * 1024 * 1024


def to_bf16(w):
    shape = w.shape
    w2 = w.reshape(-1, shape[-1])
    r, c = w2.shape
    tr = r
    for cand in (1024, 512, 256, 128, 64, 32, 16):
        if r % cand == 0 and cand * c * 4 <= CAST_BLOCK_BYTES:
            tr = cand
            break
    out = pl.pallas_call(
        _cast_kernel, grid=(r // tr,),
        in_specs=[pl.BlockSpec((tr, c), lambda i: (i, 0))],
        out_specs=pl.BlockSpec((tr, c), lambda i: (i, 0)),
        out_shape=jax.ShapeDtypeStruct((r, c), BF16),
        compiler_params=_params("parallel"), name="to_bf16",
    )(w2)
    return out.reshape(shape)


def _passenger_specs(weights, grid):
    steps, nj = grid[0] * grid[1], grid[1]
    flat, in_specs, out_specs, out_shapes = [], [], [], []
    for w in weights:
        w2 = w.reshape(-1, w.shape[-1])
        r, c = w2.shape
        tr = next(cand for cand in range(16, r + 1, 16) if r % cand == 0 and r // cand <= steps)
        last = r // tr - 1
        imap = lambda i, j, last=last: (jnp.minimum(i * nj + j, last), 0)
        flat.append(w2)
        in_specs.append(pl.BlockSpec((tr, c), imap))
        out_specs.append(pl.BlockSpec((tr, c), imap))
        out_shapes.append(jax.ShapeDtypeStruct((r, c), BF16))
    return flat, in_specs, out_specs, out_shapes


def _with_passengers(body, n_in, n_out, n_p):
    def wrapped(*refs):
        ins, p_ins = refs[:n_in], refs[n_in:n_in + n_p]
        outs = refs[n_in + n_p:n_in + n_p + n_out]
        p_outs = refs[n_in + n_p + n_out:n_in + 2 * n_p + n_out]
        scratch = refs[n_in + 2 * n_p + n_out:]
        for p_in, p_out in zip(p_ins, p_outs):
            p_out[...] = p_in[...].astype(p_out.dtype)
        body(*ins, *outs, *scratch)
    return wrapped


def _hosted_call(body, grid, in_specs, out_specs, out_shapes, args, *, sem, name, scratch_shapes=(), cast=()):
    flat, p_in, p_out, p_shapes = _passenger_specs(cast, grid)
    if cast:
        sem = ("arbitrary",) * len(sem)
    outs = pl.pallas_call(
        _with_passengers(body, len(in_specs), len(out_specs), len(flat)), grid=grid,
        in_specs=list(in_specs) + p_in, out_specs=list(out_specs) + p_out,
        out_shape=list(out_shapes) + p_shapes, scratch_shapes=list(scratch_shapes),
        compiler_params=_params(*sem), name=name)(*args, *flat)
    n_out = len(out_specs)
    return tuple(outs[:n_out]), [o.reshape(w.shape) for o, w in zip(outs[n_out:], cast)]


def _mm_kernel(a_ref, b_ref, o_ref):
    o_ref[...] = _dot(a_ref[...], b_ref[...]).astype(o_ref.dtype)


def matmul(a, b, out_dtype, tm, tn, col0=0, n=None, cast=()):
    m, k = a.shape
    n = b.shape[1] if n is None else n
    tm, tn = min(tm, m), min(tn, n)
    cb0 = col0 // tn
    (out,), casted = _hosted_call(
        _mm_kernel, (m // tm, n // tn),
        [pl.BlockSpec((tm, k), lambda i, j: (i, 0)), pl.BlockSpec((k, tn), lambda i, j: (0, cb0 + j))],
        [pl.BlockSpec((tm, tn), lambda i, j: (i, j))], [jax.ShapeDtypeStruct((m, n), out_dtype)], (a, b),
        sem=("parallel", "parallel"), name="matmul", cast=cast)
    return (out, casted) if cast else out


def _mm_res_kernel(a_ref, b_ref, r_ref, o_ref):
    o_ref[...] = r_ref[...] + _dot(a_ref[...], b_ref[...])


def matmul_residual_wide_k(a, b, res, tm, tn, cast=()):
    m, k = a.shape
    n = b.shape[1]
    tm, tn = min(tm, m), min(tn, n)
    (out,), casted = _hosted_call(
        _mm_res_kernel, (n // tn, m // tm),
        [pl.BlockSpec((tm, k), lambda j, i: (i, 0)),
         pl.BlockSpec((k, tn), lambda j, i: (0, j), pipeline_mode=pl.Buffered(1)),
         pl.BlockSpec((tm, tn), lambda j, i: (i, j))],
        [pl.BlockSpec((tm, tn), lambda j, i: (i, j))], [jax.ShapeDtypeStruct((m, n), F32)], (a, b, res),
        sem=("parallel", "parallel"), name="matmul_residual_wide_k", cast=cast)
    return (out, casted) if cast else out


def _mm_res_norm_kernel(a_ref, b_ref, r_ref, g_ref, o_ref, hg_ref, rs_ref, ssq_ref, *, d):
    j = pl.program_id(1)
    h = r_ref[...] + _dot(a_ref[...], b_ref[...])
    o_ref[...] = h
    hg_ref[...] = (h * g_ref[...]).astype(hg_ref.dtype)
    part = jnp.sum(h * h, axis=1, keepdims=True)

    @pl.when(j == 0)
    def _():
        ssq_ref[...] = part

    @pl.when(j != 0)
    def _():
        ssq_ref[...] += part

    @pl.when(j == pl.num_programs(1) - 1)
    def _():
        rs_ref[...] = lax.rsqrt(ssq_ref[...] / d + EPS)


def matmul_residual_norm(a, b, res, norm_g, tm, tn):
    m, k = a.shape
    n = b.shape[1]
    tm, tn = min(tm, m), min(tn, n)
    return pl.pallas_call(
        functools.partial(_mm_res_norm_kernel, d=n), grid=(m // tm, n // tn),
        in_specs=[pl.BlockSpec((tm, k), lambda i, j: (i, 0)), pl.BlockSpec((k, tn), lambda i, j: (0, j)),
                  pl.BlockSpec((tm, tn), lambda i, j: (i, j)), pl.BlockSpec((1, tn), lambda i, j: (0, j))],
        out_specs=[pl.BlockSpec((tm, tn), lambda i, j: (i, j)), pl.BlockSpec((tm, tn), lambda i, j: (i, j)),
                   pl.BlockSpec((tm, 1), lambda i, j: (i, 0))],
        out_shape=[jax.ShapeDtypeStruct((m, n), F32), jax.ShapeDtypeStruct((m, n), BF16),
                   jax.ShapeDtypeStruct((m, 1), F32)],
        scratch_shapes=[pltpu.VMEM((tm, 1), F32)],
        compiler_params=_params("parallel", "arbitrary"), name="matmul_residual_norm",
    )(a, b, res, norm_g.reshape(1, n))


def matmul_residual(a, b, res, tm, tn, cast=()):
    m, k = a.shape
    n = b.shape[1]
    tm, tn = min(tm, m), min(tn, n)
    (out,), casted = _hosted_call(
        _mm_res_kernel, (m // tm, n // tn),
        [pl.BlockSpec((tm, k), lambda i, j: (i, 0)), pl.BlockSpec((k, tn), lambda i, j: (0, j)),
         pl.BlockSpec((tm, tn), lambda i, j: (i, j))],
        [pl.BlockSpec((tm, tn), lambda i, j: (i, j))], [jax.ShapeDtypeStruct((m, n), F32)], (a, b, res),
        sem=("parallel", "parallel"), name="matmul_residual", cast=cast)
    return (out, casted) if cast else out


def _swiglu_kernel(a_ref, rs_ref, wg_ref, wu_ref, o_ref):
    a = a_ref[...]
    rs = rs_ref[...]
    g = _dot(a, wg_ref[...]) * rs
    u = _dot(a, wu_ref[...]) * rs
    o_ref[...] = (_silu(g) * u).astype(o_ref.dtype)


def swiglu_up(a, row_scale, w_gate, w_up, tm, tn, cast=()):
    m, k = a.shape
    n = w_gate.shape[1]
    tm, tn = min(tm, m), min(tn, n)
    (out,), casted = _hosted_call(
        _swiglu_kernel, (m // tm, n // tn),
        [pl.BlockSpec((tm, k), lambda i, j: (i, 0)), pl.BlockSpec((tm, 1), lambda i, j: (i, 0)),
         pl.BlockSpec((k, tn), lambda i, j: (0, j)), pl.BlockSpec((k, tn), lambda i, j: (0, j))],
        [pl.BlockSpec((tm, tn), lambda i, j: (i, j))], [jax.ShapeDtypeStruct((m, n), BF16)],
        (a, row_scale, w_gate, w_up), sem=("parallel", "parallel"), name="swiglu_up", cast=cast)
    return (out, casted) if cast else out


def _moe_dispatch_kernel(s1_ref, s2_ref, u3_ref, xs_in_ref, xs_ref, sem, *, r):
    del xs_in_ref
    base = pl.program_id(0) * r

    def copies(k):
        t = base + k
        return [pltpu.make_async_copy(u3_ref.at[k], xs_ref.at[s1_ref[t]], sem),
                pltpu.make_async_copy(u3_ref.at[k], xs_ref.at[s2_ref[t]], sem)]

    def start(k, c):
        for cp in copies(k):
            cp.start()
        return c

    def wait(k, c):
        for cp in copies(k):
            cp.wait()
        return c

    lax.fori_loop(0, r, start, 0)
    lax.fori_loop(0, r, wait, 0)


def moe_dispatch(u3, slot1, slot2, n_slots, r=256):
    t, s, l = u3.shape
    r = min(r, t)
    return pl.pallas_call(
        functools.partial(_moe_dispatch_kernel, r=r),
        grid_spec=pltpu.PrefetchScalarGridSpec(
            num_scalar_prefetch=2, grid=(t // r,),
            in_specs=[pl.BlockSpec((r, s, l), lambda i, s1, s2: (i, 0, 0)), pl.BlockSpec(memory_space=pl.ANY)],
            out_specs=pl.BlockSpec(memory_space=pl.ANY),
            scratch_shapes=[pltpu.SemaphoreType.DMA(())]),
        out_shape=jax.ShapeDtypeStruct((n_slots, s, l), u3.dtype),
        input_output_aliases={3: 0},
        compiler_params=_params("arbitrary"), name="moe_dispatch",
    )(slot1, slot2, u3, jnp.zeros((n_slots, s, l), u3.dtype))


def _moe_up_kernel(te_ref, nu_ref, a3_ref, wg_ref, wu_ref, o_ref, a2_ref):
    i = pl.program_id(0)
    j = pl.program_id(1)

    @pl.when(j == 0)
    def _():
        a2_ref[...] = a3_ref[...].reshape(a2_ref.shape)

    @pl.when(i < nu_ref[0])
    def _():
        a = a2_ref[...]
        g = _dot(a, wg_ref[...])
        u = _dot(a, wu_ref[...])
        o_ref[...] = (_silu(g) * u).astype(o_ref.dtype)

    @pl.when(i >= nu_ref[0])
    def _():
        o_ref[...] = jnp.zeros_like(o_ref)


def moe_up(xs3, tile_expert, n_used, w_gate, w_up, tmg, tn):
    p, s, l = xs3.shape
    n_e, k, f = w_gate.shape
    tn = min(tn, f)
    wspec = pl.BlockSpec((None, k, tn), lambda i, j, te, nu: (te[i], 0, j))
    return pl.pallas_call(
        _moe_up_kernel,
        grid_spec=pltpu.PrefetchScalarGridSpec(
            num_scalar_prefetch=2, grid=(p // tmg, f // tn),
            in_specs=[pl.BlockSpec((tmg, s, l), lambda i, j, te, nu: (i, 0, 0)), wspec, wspec],
            out_specs=pl.BlockSpec((tmg, tn), lambda i, j, te, nu: (i, j)),
            scratch_shapes=[pltpu.VMEM((tmg, k), xs3.dtype)]),
        out_shape=jax.ShapeDtypeStruct((p, f), BF16),
        compiler_params=_params("arbitrary", "arbitrary"), name="moe_up",
    )(tile_expert, n_used, xs3, w_gate, w_up)


def _moe_down_kernel(te_ref, nu_ref, a_ref, wd_ref, o3_ref):
    i = pl.program_id(0)

    @pl.when(i < nu_ref[0])
    def _():
        o3_ref[...] = _dot(a_ref[...], wd_ref[...]).astype(o3_ref.dtype).reshape(o3_ref.shape)

    @pl.when(i >= nu_ref[0])
    def _():
        o3_ref[...] = jnp.zeros_like(o3_ref)


def moe_down(hid, tile_expert, n_used, w_down, tmg):
    p, f = hid.shape
    n_e, _, d = w_down.shape
    return pl.pallas_call(
        _moe_down_kernel,
        grid_spec=pltpu.PrefetchScalarGridSpec(
            num_scalar_prefetch=2, grid=(p // tmg,),
            in_specs=[pl.BlockSpec((tmg, f), lambda i, te, nu: (i, 0)),
                      pl.BlockSpec((None, f, d), lambda i, te, nu: (te[i], 0, 0))],
            out_specs=pl.BlockSpec((tmg, d // LANES, LANES), lambda i, te, nu: (i, 0, 0))),
        out_shape=jax.ShapeDtypeStruct((p, d // LANES, LANES), BF16),
        compiler_params=_params("arbitrary"), name="moe_down",
    )(tile_expert, n_used, hid, w_down)


def _moe_combine_kernel(s1_ref, s2_ref, h_ref, meta_ref, y3_ref, g_ref, o_ref, ya_ref, yb_ref, sem, *, r, norm_out):
    base = pl.program_id(0) * r

    def copies(k):
        t = base + k
        return [pltpu.make_async_copy(y3_ref.at[s1_ref[t]], ya_ref.at[k], sem),
                pltpu.make_async_copy(y3_ref.at[s2_ref[t]], yb_ref.at[k], sem)]

    def start(k, c):
        for cp in copies(k):
            cp.start()
        return c

    def wait(k, c):
        for cp in copies(k):
            cp.wait()
        return c

    lax.fori_loop(0, r, start, 0)
    meta = meta_ref[...]
    lane = lax.broadcasted_iota(jnp.int32, meta.shape, 1)
    w1 = _lane_col(meta, lane, META_W1)
    w2 = _lane_col(meta, lane, META_W2)
    lax.fori_loop(0, r, wait, 0)
    d = h_ref.shape[1]
    ya = ya_ref[...].reshape(r, d).astype(F32)
    yb = yb_ref[...].reshape(r, d).astype(F32)
    h_new = h_ref[...] + w1 * ya + w2 * yb
    o_ref[...] = _rms(h_new, g_ref[...]) if norm_out else h_new


def moe_combine(h, meta, y3, slot1, slot2, norm_g=None, r=256):
    t, d = h.shape
    norm_out = norm_g is not None
    g = norm_g if norm_out else jnp.ones((d,), F32)
    _, s, l = y3.shape
    r = min(r, t)
    return pl.pallas_call(
        functools.partial(_moe_combine_kernel, r=r, norm_out=norm_out),
        grid_spec=pltpu.PrefetchScalarGridSpec(
            num_scalar_prefetch=2, grid=(t // r,),
            in_specs=[pl.BlockSpec((r, d), lambda i, s1, s2: (i, 0)),
                      pl.BlockSpec((r, LANES), lambda i, s1, s2: (i, 0)),
                      pl.BlockSpec(memory_space=pl.ANY),
                      pl.BlockSpec((1, d), lambda i, s1, s2: (0, 0))],
            out_specs=pl.BlockSpec((r, d), lambda i, s1, s2: (i, 0)),
            scratch_shapes=[pltpu.VMEM((r, s, l), y3.dtype), pltpu.VMEM((r, s, l), y3.dtype),
                            pltpu.SemaphoreType.DMA(())]),
        out_shape=jax.ShapeDtypeStruct((t, d), F32),
        compiler_params=_params("arbitrary"), name="moe_combine",
    )(slot1, slot2, h, meta, y3, g.reshape(1, d))


def _moe_plan(meta, counts, n_experts, tmg, n_tiles):
    i32 = jnp.int32
    e1, e2 = meta[:, META_E1].astype(i32), meta[:, META_E2].astype(i32)
    r1, r2 = meta[:, META_RANK1].astype(i32), meta[:, META_RANK2].astype(i32)
    cnt = counts[0, :n_experts].astype(i32)
    padded = (cnt + tmg - 1) // tmg * tmg
    ends = jnp.cumsum(padded)
    starts = ends - padded
    slot1 = starts[e1] + r1
    slot2 = starts[e2] + r2
    tile_row0 = jnp.arange(n_tiles, dtype=i32) * tmg
    tile_expert = jnp.minimum(jnp.sum(ends[None, :] <= tile_row0[:, None], axis=1), n_experts - 1).astype(i32)
    n_used = (ends[-1:] // tmg).astype(i32)
    return slot1, slot2, tile_expert, n_used


def _softplus(x):
    return jnp.maximum(x, 0.0) + jnp.log1p(jnp.exp(-jnp.abs(x)))


def _ssd_in_kernel(a_ref, w_ref, cw_ref, cb_ref, o_ref, carry_ref, buf_ref, *, seq_tiles, k_conv, n_plain):
    i = pl.program_id(0)
    j = pl.program_id(1)
    tm = a_ref.shape[0]
    hl = carry_ref.shape[1]
    acc = _dot(a_ref[...], w_ref[...])

    @pl.when(j < n_plain)
    def _():
        o_ref[...] = acc.astype(o_ref.dtype)

    @pl.when(j >= n_plain)
    def _():
        jc = j - n_plain

        @pl.when(i % seq_tiles == 0)
        def _():
            buf_ref[0:hl, :] = jnp.zeros((hl, acc.shape[1]), F32)

        @pl.when(i % seq_tiles != 0)
        def _():
            buf_ref[0:hl, :] = carry_ref[jc]

        buf_ref[hl:hl + tm, :] = acc
        carry_ref[jc] = acc[tm - hl:, :]
        out = acc * cw_ref[k_conv - 1:k_conv, :] + cb_ref[...]
        for sft in range(1, k_conv):
            out = out + buf_ref[hl - sft:hl - sft + tm, :] * cw_ref[k_conv - 1 - sft:k_conv - sft, :]
        o_ref[...] = _silu(out).astype(o_ref.dtype)


def ssd_in_proj(a, w, conv_w, conv_b, n_plain_cols, seq, tm=1024, tn=1024, hl=8, cast=()):
    m, k = a.shape
    k_conv, n_conv_cols = conv_w.shape
    tm, tn = min(tm, seq), min(tn, n_conv_cols)
    n = n_plain_cols + n_conv_cols
    n_plain = n_plain_cols // tn
    cmap = lambda i, j: (0, jnp.maximum(j - n_plain, 0))
    (out,), casted = _hosted_call(
        functools.partial(_ssd_in_kernel, seq_tiles=seq // tm, k_conv=k_conv, n_plain=n_plain),
        (m // tm, n // tn),
        [pl.BlockSpec((tm, k), lambda i, j: (i, 0)), pl.BlockSpec((k, tn), lambda i, j: (0, j)),
         pl.BlockSpec((k_conv, tn), cmap), pl.BlockSpec((1, tn), cmap)],
        [pl.BlockSpec((tm, tn), lambda i, j: (i, j))], [jax.ShapeDtypeStruct((m, n), BF16)],
        (a, w, conv_w, conv_b.reshape(1, n_conv_cols)),
        scratch_shapes=[pltpu.VMEM((n_conv_cols // tn, hl, tn), F32), pltpu.VMEM((hl + tm, tn), F32)],
        sem=("arbitrary", "arbitrary"), name="ssd_in_proj", cast=cast)
    return (out, casted) if cast else out


def _ssd_scan_kernel(x_ref, b_ref, c_ref, z_ref, dt_ref, dtb_ref, alog_ref, dsk_ref, ng_ref, e_ref,
                     o_ref, state_ref, acum_ref, acumT_ref, dtT_ref, we_ref, y_ref, *, hg):
    c = pl.program_id(1)
    g = pl.program_id(2)
    q = x_ref.shape[0]
    row = lax.broadcasted_iota(jnp.int32, (q, q), 0)
    col = lax.broadcasted_iota(jnp.int32, (q, q), 1)
    causal = row >= col

    @pl.when(c == 0)
    def _():
        state_ref[g] = jnp.zeros(state_ref.shape[1:], F32)

    @pl.when(g == 0)
    def _():
        dt = _softplus(dt_ref[...] + dtb_ref[...])
        a = dt * (-jnp.exp(alog_ref[...]))
        tri = jnp.where(causal, 1.0, 0.0).astype(BF16)
        a_hi, a_mid, a_lo = _split3(a)
        acum = _dot(tri, a_hi) + _dot(tri, a_mid) + _dot(tri, a_lo)
        acum_ref[...] = acum
        acumT_ref[...] = acum.T
        dtT_ref[...] = dt.T
        w_in = jnp.exp(acum[q - 1:q, :] - acum) * dt
        we_ref[...] = jnp.concatenate([w_in, jnp.exp(acum)], axis=0).astype(BF16)

    acum = acum_ref[...]
    expd = _dot(we_ref[...], e_ref[...])
    w_exp = expd[:q, :]
    e_exp = expd[q:, :]

    bm = b_ref[...]
    cm = c_ref[...]
    cb = _dot_nt(cm, bm)
    lane = lax.broadcasted_iota(jnp.int32, (q, LANES), 1)
    heads_per_tile = LANES // SSD_HEAD_DIM
    for p in range(hg // heads_per_tile):
        ms = []
        for hh in range(heads_per_tile):
            hidx = g * hg + p * heads_per_tile + hh
            colv = jnp.sum(jnp.where(lane == hidx, acum, 0.0), axis=1, keepdims=True)
            rowv = acumT_ref[pl.ds(hidx, 1), :]
            dtr = dtT_ref[pl.ds(hidx, 1), :]
            decay = jnp.where(causal, jnp.exp(jnp.minimum(colv - rowv, 0.0)), 0.0)
            ms.append((cb * decay * dtr).astype(BF16))
        m_cat = jnp.concatenate(ms, axis=1)
        xp = x_ref[:, p * LANES:(p + 1) * LANES].astype(F32)
        x_bd = jnp.concatenate(
            [jnp.where((lane // SSD_HEAD_DIM) == hh, xp, 0.0).astype(BF16) for hh in range(heads_per_tile)],
            axis=0)
        y_ref[:, p * LANES:(p + 1) * LANES] = _dot(m_cat, x_bd)

    st = state_ref[g]
    xf = x_ref[...].astype(F32)
    y_off = _dot(cm, st.astype(BF16)) * e_exp
    xw = (xf * w_exp).astype(BF16)
    bt = bm.astype(F32).T.astype(BF16)
    state_ref[g] = st * e_exp[q - 1:q, :] + _dot(bt, xw)
    y = y_ref[...] + y_off + xf * dsk_ref[...]
    y = y * _silu(z_ref[...].astype(F32))
    y = y * lax.rsqrt(jnp.mean(y * y, axis=-1, keepdims=True) + EPS) * ng_ref[...]
    o_ref[...] = y.astype(o_ref.dtype)


def ssd_scan(zx, dt_raw, dt_bias, a_log, d_skip, norm_g, batch, seq):
    t = zx.shape[0]
    heads = dt_raw.shape[1]
    assert heads == LANES
    hg = heads // SSD_GROUPS
    gw = hg * SSD_HEAD_DIM
    d_inner = heads * SSD_HEAD_DIM
    q = SSD_CHUNK
    nc = seq // q
    n = SSD_STATE
    assert gw % LANES == 0 and n == LANES
    x_blk0 = d_inner // gw
    b_blk0 = 2 * d_inner // n
    c_blk0 = (2 * d_inner + SSD_GROUPS * n) // n
    ch = jnp.arange(gw) // SSD_HEAD_DIM
    expand = (jnp.arange(heads)[None, :, None] ==
              (jnp.arange(SSD_GROUPS)[:, None, None] * hg + ch[None, None, :])).astype(BF16)
    d_exp = jnp.repeat(d_skip, SSD_HEAD_DIM).reshape(1, d_inner)
    rowmap = lambda b, c, g: (b * nc + c)
    return pl.pallas_call(
        functools.partial(_ssd_scan_kernel, hg=hg),
        grid=(batch, nc, SSD_GROUPS),
        in_specs=[pl.BlockSpec((q, gw), lambda b, c, g: (rowmap(b, c, g), x_blk0 + g)),
                  pl.BlockSpec((q, n), lambda b, c, g: (rowmap(b, c, g), b_blk0 + g)),
                  pl.BlockSpec((q, n), lambda b, c, g: (rowmap(b, c, g), c_blk0 + g)),
                  pl.BlockSpec((q, gw), lambda b, c, g: (rowmap(b, c, g), g)),
                  pl.BlockSpec((q, heads), lambda b, c, g: (rowmap(b, c, g), 0)),
                  pl.BlockSpec((1, heads), lambda b, c, g: (0, 0)),
                  pl.BlockSpec((1, heads), lambda b, c, g: (0, 0)),
                  pl.BlockSpec((1, gw), lambda b, c, g: (0, g)),
                  pl.BlockSpec((1, gw), lambda b, c, g: (0, g)),
                  pl.BlockSpec((None, heads, gw), lambda b, c, g: (g, 0, 0))],
        out_specs=pl.BlockSpec((q, gw), lambda b, c, g: (rowmap(b, c, g), g)),
        out_shape=jax.ShapeDtypeStruct((t, d_inner), BF16),
        scratch_shapes=[pltpu.VMEM((SSD_GROUPS, n, gw), F32), pltpu.VMEM((q, heads), F32),
                        pltpu.VMEM((heads, q), F32), pltpu.VMEM((heads, q), F32),
                        pltpu.VMEM((2 * q, heads), BF16), pltpu.VMEM((q, gw), F32)],
        compiler_params=_params("parallel", "arbitrary", "arbitrary"), name="ssd_scan",
    )(zx, zx, zx, zx, dt_raw, dt_bias.reshape(1, heads), a_log.reshape(1, heads), d_exp,
      norm_g.reshape(1, d_inner), expand)


def _shortconv_in_kernel(a_ref, wb_ref, wc_ref, wh_ref, cw_ref, o_ref, carry_ref, buf_ref, *, seq_tiles, k_conv):
    i = pl.program_id(0)
    j = pl.program_id(1)
    tm = a_ref.shape[0]
    hl = carry_ref.shape[1]
    a = a_ref[...]
    gate_b = _dot(a, wb_ref[...])
    v = _dot(a, wc_ref[...]) * _dot(a, wh_ref[...])

    @pl.when(i % seq_tiles == 0)
    def _():
        buf_ref[0:hl, :] = jnp.zeros((hl, v.shape[1]), F32)

    @pl.when(i % seq_tiles != 0)
    def _():
        buf_ref[0:hl, :] = carry_ref[j]

    buf_ref[hl:hl + tm, :] = v
    carry_ref[j] = v[tm - hl:, :]
    acc = v * cw_ref[k_conv - 1:k_conv, :]
    for s in range(1, k_conv):
        acc = acc + buf_ref[hl - s:hl - s + tm, :] * cw_ref[k_conv - 1 - s:k_conv - s, :]
    o_ref[...] = (gate_b * acc).astype(o_ref.dtype)


def shortconv_in(a, w_in, conv_w, seq, tm=1024, tn=256, hl=8, cast=()):
    m, k = a.shape
    n = w_in.shape[1] // 3
    k_conv = conv_w.shape[0]
    tm, tn = min(tm, seq), min(tn, n)
    nj = n // tn
    wspecs = [pl.BlockSpec((k, tn), lambda i, j, part=part: (0, part * nj + j)) for part in range(3)]
    (out,), casted = _hosted_call(
        functools.partial(_shortconv_in_kernel, seq_tiles=seq // tm, k_conv=k_conv),
        (m // tm, n // tn),
        [pl.BlockSpec((tm, k), lambda i, j: (i, 0)), *wspecs, pl.BlockSpec((k_conv, tn), lambda i, j: (0, j))],
        [pl.BlockSpec((tm, tn), lambda i, j: (i, j))], [jax.ShapeDtypeStruct((m, n), BF16)],
        (a, w_in, w_in, w_in, conv_w),
        scratch_shapes=[pltpu.VMEM((n // tn, hl, tn), F32), pltpu.VMEM((hl + tm, tn), F32)],
        sem=("arbitrary", "arbitrary"), name="shortconv_in", cast=cast)
    return (out, casted) if cast else out


def _pool_kernel(h_ref, halo_ref, g_ref, w_ref, sc_ref, g2_ref, o_ref, u_ref, buf_ref, *, seq, windows):
    i = pl.program_id(0)
    tm, d = h_ref.shape
    hl = halo_ref.shape[0]
    gw = d // len(windows)
    h = h_ref[...]
    g = g_ref[...]
    xn = _rms(h, g)
    row0 = i * tm
    seq_start = (row0 % seq) == 0
    buf_ref[0:hl, :] = jnp.where(seq_start, 0.0, _rms(halo_ref[...], g))
    buf_ref[hl:hl + tm, :] = xn
    pos = (row0 + lax.broadcasted_iota(jnp.int32, (tm, 1), 0)) % seq
    for gi, w in enumerate(windows):
        cs = slice(gi * gw, (gi + 1) * gw)
        tok = xn[:, cs]
        acc = tok
        for s in range(1, w):
            acc = acc + buf_ref[hl - s:hl - s + tm, cs]
        cnt = jnp.minimum(pos + 1, w).astype(F32)
        pooled = (acc / cnt - tok).astype(BF16)
        y = _dot(pooled, w_ref[gi])
        o_ref[:, cs] = h[:, cs] + y * sc_ref[:, cs]
    u_ref[...] = _rms(o_ref[...], g2_ref[...]).astype(u_ref.dtype)


def pool_mixer_residual(h, norm_g, w_group, scale, next_norm_g, seq, tm=256):
    t, d = h.shape
    tm = min(tm, seq)
    hl = POOL_HALO
    n_g, gw, _ = w_group.shape
    return pl.pallas_call(
        functools.partial(_pool_kernel, seq=seq, windows=POOL_WINDOWS),
        grid=(t // tm,),
        in_specs=[pl.BlockSpec((tm, d), lambda i: (i, 0)),
                  pl.BlockSpec((hl, d), lambda i: (jnp.maximum(i * (tm // hl) - 1, 0), 0)),
                  pl.BlockSpec((1, d), lambda i: (0, 0)),
                  pl.BlockSpec((n_g, gw, gw), lambda i: (0, 0, 0)),
                  pl.BlockSpec((1, d), lambda i: (0, 0)),
                  pl.BlockSpec((1, d), lambda i: (0, 0))],
        out_specs=[pl.BlockSpec((tm, d), lambda i: (i, 0)), pl.BlockSpec((tm, d), lambda i: (i, 0))],
        out_shape=[jax.ShapeDtypeStruct((t, d), F32), jax.ShapeDtypeStruct((t, d), BF16)],
        scratch_shapes=[pltpu.VMEM((hl + tm, d), F32)],
        compiler_params=_params("parallel"), name="pool_mixer",
    )(h, h, norm_g.reshape(1, d), w_group, scale.reshape(1, d), next_norm_g.reshape(1, d))


def _retention_kernel(q_ref, k_ref, v_ref, g_ref, cos_ref, sin_ref, inner_ref, qd_ref, kd_ref, cd_ref,
                      o_ref, state_ref):
    c = pl.program_id(2)

    @pl.when(c == 0)
    def _():
        state_ref[...] = jnp.zeros_like(state_ref)

    dk = q_ref.shape[1]
    half = dk // 2
    q = inner_ref.shape[0]
    st = state_ref[...]
    for cc in range(q_ref.shape[0] // q):
        rows = slice(cc * q, (cc + 1) * q)
        cos = cos_ref[rows, :]
        sin = sin_ref[rows, :]

        def rot(t):
            t1, t2 = t[:, :half], t[:, half:]
            return jnp.concatenate([t1 * cos - t2 * sin, t1 * sin + t2 * cos], axis=1)

        qr = rot(q_ref[rows, :].astype(F32))
        kr = rot(k_ref[rows, :].astype(F32)) * (dk ** -0.5)
        v = v_ref[rows, :]
        scores = _dot_nt(qr.astype(BF16), kr.astype(BF16)) * inner_ref[...]
        y = _dot(scores.astype(BF16), v) + _dot((qr * qd_ref[...]).astype(BF16), st.astype(BF16))
        kdt = (kr * kd_ref[...]).T.astype(BF16)
        st = st * cd_ref[...] + _dot(kdt, v)
        y = y * lax.rsqrt(jnp.mean(y * y, axis=-1, keepdims=True) + EPS)
        o_ref[rows, :] = (y * _silu(g_ref[rows, :].astype(F32))).astype(o_ref.dtype)
    state_ref[...] = st


def retention(qkvg, batch, seq, chunks_per_step=16):
    t, n_in = qkvg.shape
    hh = RET_HEADS
    dk = n_in // (6 * hh)
    dv = 2 * dk
    q = RET_CHUNK
    rb = q * min(chunks_per_step, seq // q)
    nc = seq // rb
    half = dk // 2
    pos = jnp.arange(seq, dtype=F32)
    inv = ROPE_BASE ** (-jnp.arange(half, dtype=F32) / half)
    ang = pos[:, None] * inv[None]
    cos, sin = jnp.cos(ang), jnp.sin(ang)
    log_gamma = jnp.log1p(-jnp.exp2(-5.0 - jnp.arange(hh, dtype=F32)))
    idx = jnp.arange(q, dtype=F32)
    rel = idx[:, None] - idx[None, :]
    inner = jnp.exp(jnp.where(rel[None] >= 0, rel[None] * log_gamma[:, None, None], -jnp.inf))
    q_decay = jnp.exp((idx + 1.0)[None, :] * log_gamma[:, None])[..., None]
    k_decay = jnp.exp((q - 1.0 - idx)[None, :] * log_gamma[:, None])[..., None]
    c_decay = jnp.exp(q * log_gamma)[:, None, None]
    rowmap = lambda b, h, c: b * nc + c
    return pl.pallas_call(
        _retention_kernel, grid=(batch, hh, nc),
        in_specs=[pl.BlockSpec((rb, dk), lambda b, h, c: (rowmap(b, h, c), h)),
                  pl.BlockSpec((rb, dk), lambda b, h, c: (rowmap(b, h, c), hh + h)),
                  pl.BlockSpec((rb, dv), lambda b, h, c: (rowmap(b, h, c), hh + h)),
                  pl.BlockSpec((rb, dv), lambda b, h, c: (rowmap(b, h, c), 2 * hh + h)),
                  pl.BlockSpec((rb, half), lambda b, h, c: (c, 0)),
                  pl.BlockSpec((rb, half), lambda b, h, c: (c, 0)),
                  pl.BlockSpec((None, q, q), lambda b, h, c: (h, 0, 0)),
                  pl.BlockSpec((None, q, 1), lambda b, h, c: (h, 0, 0)),
                  pl.BlockSpec((None, q, 1), lambda b, h, c: (h, 0, 0)),
                  pl.BlockSpec((None, 1, 1), lambda b, h, c: (h, 0, 0))],
        out_specs=pl.BlockSpec((rb, dv), lambda b, h, c: (rowmap(b, h, c), h)),
        out_shape=jax.ShapeDtypeStruct((t, hh * dv), BF16),
        scratch_shapes=[pltpu.VMEM((dk, dv), F32)],
        compiler_params=_params("parallel", "parallel", "arbitrary"), name="retention",
    )(qkvg, qkvg, qkvg, qkvg, cos, sin, inner, q_decay, k_decay, c_decay)


def _moe_ffn(h, norm_g, w_router, w_gate, w_up, w_down, out_norm_g=None, tmg=512):
    t, d = h.shape
    n_e = w_router.shape[1]
    tmg = min(tmg, t)
    n_tiles = (2 * t) // tmg + n_e
    u3, meta, counts = rmsnorm_router(h, norm_g, w_router)
    slot1, slot2, tile_expert, n_used = _moe_plan(meta, counts, n_e, tmg, n_tiles)
    xs3 = moe_dispatch(u3, slot1, slot2, n_tiles * tmg)
    hid = moe_up(xs3, tile_expert, n_used, w_gate, w_up, tmg, tn=768)
    y3 = moe_down(hid, tile_expert, n_used, w_down, tmg)
    return moe_combine(h, meta, y3, slot1, slot2, norm_g=out_norm_g)


def kernel(x, norm_mix, norm_ffn, norm_final, ssd_w_in, ssd_conv_w, ssd_conv_b, ssd_dt_bias, ssd_a_log, ssd_d, ssd_norm, ssd_w_out, sc_w_in, sc_conv_w, sc_w_out, pool_w, pool_scale, ret_w_qkvg, ret_w_out, ffn0_w_gate, ffn0_w_up, ffn0_w_down, moe1_router, moe1_w_gate, moe1_w_up, moe1_w_down, ffn2_w_gate, ffn2_w_up, ffn2_w_down, moe3_router, moe3_w_gate, moe3_w_up, moe3_w_down):
    batch, seq, d = x.shape
    t = batch * seq
    h = x.reshape(t, d)
    unit_scale = jnp.ones((t, 1), F32)

    heads = ssd_dt_bias.shape[0]
    d_inner = heads * SSD_HEAD_DIM
    n_xbc = ssd_conv_w.shape[1]
    u = rmsnorm(h, norm_mix[0], BF16)
    w_in = to_bf16(ssd_w_in)
    zx, (w_ssd_out, w_g0, w_u0) = ssd_in_proj(u, w_in, ssd_conv_w, ssd_conv_b, d_inner, seq,
                                              cast=(ssd_w_out, ffn0_w_gate, ffn0_w_up))
    dt_raw = matmul(u, w_in, F32, tm=1024, tn=LANES, col0=d_inner + n_xbc, n=heads)
    y = ssd_scan(zx, dt_raw, ssd_dt_bias, ssd_a_log, ssd_d, ssd_norm, batch, seq)
    h, hg, rs = matmul_residual_norm(y, w_ssd_out, h, norm_ffn[0], tm=1024, tn=256)
    hid, (w_d0, w_sc_in, w_sc_out) = swiglu_up(hg, rs, w_g0, w_u0, tm=2048, tn=256,
                                               cast=(ffn0_w_down, sc_w_in, sc_w_out))
    h, (w_g1, w_u2) = matmul_residual_wide_k(hid, w_d0, h, tm=512, tn=1024, cast=(moe1_w_gate, ffn2_w_up))

    u = rmsnorm(h, norm_mix[1], BF16)
    y, (w_u1, w_d1) = shortconv_in(u, w_sc_in, sc_conv_w, seq, cast=(moe1_w_up, moe1_w_down))
    h, (w_g2,) = matmul_residual_wide_k(y, w_sc_out, h, tm=1024, tn=1024, cast=(ffn2_w_gate,))
    h = _moe_ffn(h, norm_ffn[1], moe1_router, w_g1, w_u1, w_d1)

    h, u = pool_mixer_residual(h, norm_mix[2], to_bf16(pool_w), pool_scale, norm_ffn[2], seq)
    hid, (w_d2, w_qkvg) = swiglu_up(u, unit_scale, w_g2, w_u2, tm=2048, tn=256, cast=(ffn2_w_down, ret_w_qkvg))
    h, (w_ret_out,) = matmul_residual_wide_k(hid, w_d2, h, tm=512, tn=1024, cast=(ret_w_out,))

    u = rmsnorm(h, norm_mix[3], BF16)
    qkvg, (w_g3, w_u3, w_d3) = matmul(u, w_qkvg, BF16, tm=1024, tn=1024, cast=(moe3_w_gate, moe3_w_up, moe3_w_down))
    y = retention(qkvg, batch, seq)
    h = matmul_residual_wide_k(y, w_ret_out, h, tm=512, tn=1024)
    out = _moe_ffn(h, norm_ffn[3], moe3_router, w_g3, w_u3, w_d3, out_norm_g=norm_final)
    return out.reshape(batch, seq, d)
```
